```python
import math
import jax, jax.numpy as jnp
from jax import lax
import numpy as np

D_MODEL = 1024
BATCH = 32
SEQ = 2048
DEPTH = 2

HEAD_DIM = 64
N_GROUP_HEADS = 4
GROUP_WIDTH = N_GROUP_HEADS * HEAD_DIM
N_MIXERS = 4
MIX_WIDTH = N_MIXERS * GROUP_WIDTH
ROPE_THETA = 500000.0
ROPE_FRACTION = 4
RMS_EPS = 1e-6

MOBA_BLOCK = 256
MOBA_TOPK = 3
MOBA_Q_CHUNK = 16
RWKV_DECAY_LORA = 64
RWKV_A_LORA = 64
RWKV_GATE_LORA = 128
RWKV_LN_EPS = 64e-5
RWKV_SPLITS = [GROUP_WIDTH, RWKV_DECAY_LORA, GROUP_WIDTH, GROUP_WIDTH, RWKV_A_LORA, RWKV_GATE_LORA]
RWKV_WIDTH = sum(RWKV_SPLITS)
SGU_CHUNK = 128
DIFF_QK_DIM = HEAD_DIM // 2
DIFF_Q_BLOCK = 128
DIFF_SUBLN_EPS = 1e-5

A_COLS = 3 * GROUP_WIDTH
B_COLS = RWKV_WIDTH
C_COLS = 2 * GROUP_WIDTH
D_COLS = 3 * GROUP_WIDTH
IN_WIDTH = A_COLS + B_COLS + C_COLS + D_COLS

N_EXPERT_GROUPS = 4
EXPERTS_PER_GROUP = 8
N_EXPERTS = N_EXPERT_GROUPS * EXPERTS_PER_GROUP
EXPERT_TOPK = 2
EXPERT_FF = D_MODEL // 2
MOE_ROW_BLOCK = 128

kernel_name = 'hybrid_moba_rwkv7_sgu_diffattn_hmoe'


def rmsnorm(x, g, eps=RMS_EPS):
    xf = x.astype(jnp.float32)
    y = xf * lax.rsqrt(jnp.mean(xf * xf, axis=-1, keepdims=True) + eps)
    return (y * g).astype(x.dtype)


def layernorm(x, g, b, eps=1e-5):
    xf = x.astype(jnp.float32)
    mu = jnp.mean(xf, axis=-1, keepdims=True)
    var = jnp.mean(jnp.square(xf - mu), axis=-1, keepdims=True)
    return ((xf - mu) * lax.rsqrt(var + eps) * g + b).astype(x.dtype)


def rope_tables(seq, rot_dim):
    inv = ROPE_THETA ** (-jnp.arange(0, rot_dim, 2, dtype=jnp.float32) / rot_dim)
    ang = jnp.arange(seq, dtype=jnp.float32)[:, None] * inv[None, :]
    return jnp.cos(ang), jnp.sin(ang)


def apply_partial_rope(t, cos, sin):
    half = cos.shape[-1]
    c = cos.astype(t.dtype)
    s = sin.astype(t.dtype)
    t1 = t[..., :half]
    t2 = t[..., half:2 * half]
    return jnp.concatenate([t1 * c - t2 * s, t2 * c + t1 * s, t[..., 2 * half:]], axis=-1)


def split_heads(t, n):
    b, s, _ = t.shape
    return t.reshape(b, s, n, -1).transpose(0, 2, 1, 3)


def merge_heads(t):
    b, h, s, d = t.shape
    return t.transpose(0, 2, 1, 3).reshape(b, s, h * d)


def moba_attention(q, k, v, cos, sin):
    B, H, S, Dh = q.shape
    q = apply_partial_rope(q, cos, sin)
    k = apply_partial_rope(k, cos, sin)
    nb = -(-S // MOBA_BLOCK)
    sp = nb * MOBA_BLOCK
    pad = ((0, 0), (0, 0), (0, sp - S), (0, 0))
    q, k, v = jnp.pad(q, pad), jnp.pad(k, pad), jnp.pad(v, pad)
    kb = k.reshape(B, H, nb, MOBA_BLOCK, Dh)
    vb = v.reshape(B, H, nb, MOBA_BLOCK, Dh)
    kmean = jnp.mean(kb.astype(jnp.float32), axis=3)
    gate = jnp.einsum('bhsd,bhnd->bhsn', q.astype(jnp.float32), kmean)
    qpos = jnp.arange(sp)
    qblk = qpos // MOBA_BLOCK
    past = jnp.arange(nb)[None, :] < qblk[:, None]
    gate = jnp.where(past, gate, -jnp.inf)
    n_sel = min(MOBA_TOPK, max(nb - 1, 1))
    _, top_idx = lax.top_k(gate, n_sel)
    top_valid = top_idx < qblk[:, None]
    own = jnp.broadcast_to(qblk[:, None], (B, H, sp, 1)).astype(top_idx.dtype)
    blk_idx = jnp.concatenate([top_idx, own], axis=-1)
    blk_valid = jnp.concatenate([top_valid, jnp.ones((B, H, sp, 1), bool)], axis=-1)

    nc = sp // MOBA_Q_CHUNK

    def to_chunks(t):
        t = t.reshape((B, H, nc, MOBA_Q_CHUNK) + t.shape[3:])
        return jnp.moveaxis(t, 2, 0)

    bi = jnp.arange(B)[:, None, None, None]
    hi = jnp.arange(H)[None, :, None, None]
    scale = Dh ** -0.5

    def attend(args):
        qc, ic, mc, pc = args
        kg = kb[bi, hi, ic]
        vg = vb[bi, hi, ic]
        s = jnp.einsum('bhcd,bhcnkd->bhcnk', qc, kg).astype(jnp.float32) * scale
        kpos = ic[..., None] * MOBA_BLOCK + jnp.arange(MOBA_BLOCK)
        mask = mc[..., None] & (kpos <= pc[None, None, :, None, None])
        s = jnp.where(mask, s, -jnp.inf)
        p = jax.nn.softmax(s.reshape(B, H, MOBA_Q_CHUNK, -1), axis=-1).reshape(s.shape)
        return jnp.einsum('bhcnk,bhcnkd->bhcd', p.astype(vg.dtype), vg)

    out = lax.map(attend, (to_chunks(q), to_chunks(blk_idx), to_chunks(blk_valid),
                           qpos.reshape(nc, MOBA_Q_CHUNK)))
    out = jnp.moveaxis(out, 0, 2).reshape(B, H, sp, Dh)[:, :, :S]
    return merge_heads(out)


def rwkv7_time_mix(proj, mu, w0, w2, a0, a2, g2, k_k, k_a, r_k, ln_g, ln_b):
    dt = proj.dtype
    B, S, _ = proj.shape
    H, N = N_GROUP_HEADS, HEAD_DIM
    prev = jnp.pad(proj, ((0, 0), (1, 0), (0, 0)))[:, :-1]
    xs = proj + (prev - proj) * mu
    r, wd, k, v, ad, gd = jnp.split(xs, list(np.cumsum(RWKV_SPLITS)[:-1]), axis=-1)
    w_log = -jax.nn.softplus(-(w0 + jnp.tanh(wd) @ w2).astype(jnp.float32)) - 0.5
    decay = jnp.exp(-jnp.exp(w_log))
    a = jax.nn.sigmoid(a0 + ad @ a2)
    g = jax.nn.sigmoid(gd) @ g2
    kk = (k * k_k).reshape(B, S, H, N).astype(jnp.float32)
    kk = kk / jnp.maximum(jnp.linalg.norm(kk, axis=-1, keepdims=True), 1e-12)
    k = k * (1.0 + (a - 1.0) * k_a)

    f32 = lambda t: t.reshape(B, S, H, N).astype(jnp.float32)
    r_h, k_h, v_h, a_h = f32(r), f32(k), f32(v), f32(a)
    w_h = decay.reshape(B, S, H, N)
    seq_first = lambda t: jnp.moveaxis(t, 1, 0)

    def step(state, inp):
        r_t, w_t, k_t, v_t, a_t, b_t = inp
        sa = jnp.einsum('bhvk,bhk->bhv', state, a_t)
        state = (state * w_t[:, :, None, :] + sa[..., None] * b_t[:, :, None, :]
                 + v_t[..., None] * k_t[:, :, None, :])
        return state, jnp.einsum('bhvk,bhk->bhv', state, r_t)

    state0 = jnp.zeros((B, H, N, N), jnp.float32)
    _, y = lax.scan(step, state0, (seq_first(r_h), seq_first(w_h), seq_first(k_h),
                                   seq_first(v_h), seq_first(-kk), seq_first(kk * a_h)))
    y = jnp.moveaxis(y, 0, 1)
    mean = jnp.mean(y, axis=-1, keepdims=True)
    var = jnp.mean(jnp.square(y - mean), axis=-1, keepdims=True)
    y = ((y - mean) * lax.rsqrt(var + RWKV_LN_EPS)).reshape(B, S, GROUP_WIDTH) * ln_g + ln_b
    bonus = jnp.sum(r_h * k_h * r_k.astype(jnp.float32), axis=-1, keepdims=True) * v_h
    return ((y + bonus.reshape(B, S, GROUP_WIDTH)) * g).astype(dt)


def sgu_mix(u, v, ln_g, ln_b, w_s, b_s):
    u = jax.nn.gelu(u, approximate=False)
    v = layernorm(jax.nn.gelu(v, approximate=False), ln_g, ln_b)
    B, S, _ = v.shape
    nc = S // SGU_CHUNK
    vh = v.reshape(B, nc, SGU_CHUNK, N_GROUP_HEADS, HEAD_DIM)
    causal = jnp.tril(jnp.ones((SGU_CHUNK, SGU_CHUNK), bool))
    w = jnp.where(causal[None], w_s, 0.0).astype(v.dtype)
    mixed = jnp.einsum('hts,bcshd->bcthd', w, vh) + b_s.T[None, None, :, :, None]
    return u * mixed.reshape(B, S, GROUP_WIDTH)


def diff_attention(q, k, v, lq1, lk1, lq2, lk2, subln_g, lambda_init, cos, sin):
    B, S, _ = q.shape
    H = N_GROUP_HEADS
    q = q.reshape(B, S, H, 2, DIFF_QK_DIM).transpose(0, 2, 3, 1, 4)
    k = k.reshape(B, S, H, 2, DIFF_QK_DIM).transpose(0, 2, 3, 1, 4)
    q = apply_partial_rope(q, cos, sin)
    k = apply_partial_rope(k, cos, sin)
    vh = split_heads(v, H)
    f = lambda t: t.astype(jnp.float32)
    lam = jnp.exp(jnp.sum(f(lq1) * f(lk1))) - jnp.exp(jnp.sum(f(lq2) * f(lk2))) + lambda_init
    scale = DIFF_QK_DIM ** -0.5
    outs = []
    for i in range(S // DIFF_Q_BLOCK):
        q0 = i * DIFF_Q_BLOCK
        end = q0 + DIFF_Q_BLOCK
        s = jnp.einsum('bhiqd,bhikd->bhiqk', q[:, :, :, q0:end], k[:, :, :, :end]).astype(jnp.float32) * scale
        mask = jnp.arange(end)[None, :] <= (q0 + jnp.arange(DIFF_Q_BLOCK))[:, None]
        p = jax.nn.softmax(jnp.where(mask, s, -jnp.inf), axis=-1)
        attn = p[:, :, 0] - lam * p[:, :, 1]
        outs.append(jnp.einsum('bhqk,bhkd->bhqd', attn.astype(vh.dtype), vh[:, :, :end]))
    o = jnp.concatenate(outs, axis=2)
    o = rmsnorm(o, subln_g, DIFF_SUBLN_EPS) * (1.0 - lambda_init)
    return merge_heads(o)


def hier_moe(x, wg, bg, we, be, w_gate, w_up, w_down):
    B, S, D = x.shape
    T = B * S
    xt = x.reshape(T, D)
    gprob = jax.nn.softmax((xt @ wg + bg).astype(jnp.float32), axis=-1)
    gp, gidx = lax.top_k(gprob, 1)
    elog = (xt @ we + be).astype(jnp.float32).reshape(T, N_EXPERT_GROUPS, EXPERTS_PER_GROUP)
    elog = jnp.take_along_axis(elog, gidx[:, :, None], axis=1)[:, 0]
    ep, eidx = lax.top_k(jax.nn.softmax(elog, axis=-1), EXPERT_TOPK)
    weight = gp * (ep / jnp.sum(ep, axis=-1, keepdims=True))
    expert = gidx * EXPERTS_PER_GROUP + eidx

    M = T * EXPERT_TOPK
    e_flat = expert.reshape(M)
    w_flat = weight.reshape(M)
    tok = jnp.repeat(jnp.arange(T), EXPERT_TOPK)
    order = jnp.argsort(e_flat)
    e_s, tok_s, w_s = e_flat[order], tok[order], w_flat[order]
    counts = jax.ops.segment_sum(jnp.ones((M,), jnp.int32), e_flat, num_segments=N_EXPERTS)
    starts = jnp.cumsum(counts) - counts
    pcounts = (counts + MOE_ROW_BLOCK - 1) // MOE_ROW_BLOCK * MOE_ROW_BLOCK
    pends = jnp.cumsum(pcounts)
    pstarts = pends - pcounts
    dest = pstarts[e_s] + (jnp.arange(M) - starts[e_s])
    P = M + N_EXPERTS * MOE_ROW_BLOCK
    nblk = P // MOE_ROW_BLOCK
    row_tok = jnp.full((P,), T, jnp.int32).at[dest].set(tok_s)
    row_w = jnp.zeros((P,), x.dtype).at[dest].set(w_s.astype(x.dtype))
    blk_expert = jnp.minimum(jnp.searchsorted(pends, jnp.arange(nblk) * MOE_ROW_BLOCK, side='right'),
                             N_EXPERTS - 1)
    xpad = jnp.concatenate([xt, jnp.zeros((1, D), xt.dtype)], axis=0)
    rows = xpad[row_tok].reshape(nblk, MOE_ROW_BLOCK, D)

    def expert_block(args):
        xb, e = args
        hdn = jax.nn.silu(xb @ w_gate[e]) * (xb @ w_up[e])
        return hdn @ w_down[e]

    y = lax.map(expert_block, (rows, blk_expert)).reshape(P, D)
    out = jnp.zeros((T + 1, D), x.dtype).at[row_tok].add(y * row_w[:, None])[:T]
    return out.reshape(B, S, D)


def setup_inputs(seed: int = 0) -> dict:
    key = jax.random.key(seed)
    ks = iter(jax.random.split(key, 40))

    def nrm(shape, scale):
        return jax.random.normal(next(ks), shape, jnp.float32) * scale

    L, D, GW, H, N = DEPTH, D_MODEL, GROUP_WIDTH, N_GROUP_HEADS, HEAD_DIM
    E, F = N_EXPERTS, EXPERT_FF
    return {
        'x': nrm((BATCH, SEQ, D), 1.0),
        'norm1_g': 1.0 + nrm((L, D), 0.02),
        'w_in': nrm((L, D, IN_WIDTH), D ** -0.5),
        'rwkv_mu': jax.random.uniform(next(ks), (L, RWKV_WIDTH), jnp.float32),
        'rwkv_w0': jax.random.uniform(next(ks), (L, GW), jnp.float32, minval=-6.0, maxval=1.0),
        'rwkv_w2': nrm((L, RWKV_DECAY_LORA, GW), 0.5 * RWKV_DECAY_LORA ** -0.5),
        'rwkv_a0': nrm((L, GW), 0.1),
        'rwkv_a2': nrm((L, RWKV_A_LORA, GW), RWKV_A_LORA ** -0.5),
        'rwkv_g2': nrm((L, RWKV_GATE_LORA, GW), RWKV_GATE_LORA ** -0.5),
        'rwkv_kk': 0.85 + nrm((L, GW), 0.02),
        'rwkv_ka': 1.0 + nrm((L, GW), 0.02),
        'rwkv_rk': nrm((L, H, N), 0.1),
        'rwkv_ln_g': 1.0 + nrm((L, GW), 0.02),
        'rwkv_ln_b': nrm((L, GW), 0.02),
        'sgu_ln_g': 1.0 + nrm((L, GW), 0.02),
        'sgu_ln_b': nrm((L, GW), 0.02),
        'sgu_w': nrm((L, H, SGU_CHUNK, SGU_CHUNK), SGU_CHUNK ** -0.5),
        'sgu_b': 1.0 + nrm((L, H, SGU_CHUNK), 0.02),
        'diff_lq1': nrm((L, DIFF_QK_DIM), 0.1),
        'diff_lk1': nrm((L, DIFF_QK_DIM), 0.1),
        'diff_lq2': nrm((L, DIFF_QK_DIM), 0.1),
        'diff_lk2': nrm((L, DIFF_QK_DIM), 0.1),
        'diff_subln_g': 1.0 + nrm((L, N), 0.02),
        'w_out': nrm((L, MIX_WIDTH, D), MIX_WIDTH ** -0.5),
        'norm2_g': 1.0 + nrm((L, D), 0.02),
        'router_group_w': nrm((L, D, N_EXPERT_GROUPS), D ** -0.5),
        'router_group_b': nrm((L, N_EXPERT_GROUPS), 0.01),
        'router_expert_w': nrm((L, D, E), D ** -0.5),
        'router_expert_b': nrm((L, E), 0.01),
        'moe_w_gate': nrm((L, E, D, F), D ** -0.5),
        'moe_w_up': nrm((L, E, D, F), D ** -0.5),
        'moe_w_down': nrm((L, E, F, D), F ** -0.5),
        'final_g': 1.0 + nrm((D,), 0.02),
    }


def reference(x, norm1_g, w_in, rwkv_mu, rwkv_w0, rwkv_w2, rwkv_a0, rwkv_a2, rwkv_g2,
              rwkv_kk, rwkv_ka, rwkv_rk, rwkv_ln_g, rwkv_ln_b, sgu_ln_g, sgu_ln_b, sgu_w, sgu_b,
              diff_lq1, diff_lk1, diff_lq2, diff_lk2, diff_subln_g, w_out, norm2_g,
              router_group_w, router_group_b, router_expert_w, router_expert_b,
              moe_w_gate, moe_w_up, moe_w_down, final_g):
    S = x.shape[1]
    cos_a, sin_a = rope_tables(S, HEAD_DIM // ROPE_FRACTION)
    cos_d, sin_d = rope_tables(S, DIFF_QK_DIM // ROPE_FRACTION)
    cut = [A_COLS, A_COLS + B_COLS, A_COLS + B_COLS + C_COLS]
    for l in range(DEPTH):
        proj = rmsnorm(x, norm1_g[l]) @ w_in[l]
        pa, pb, pc, pd = jnp.split(proj, cut, axis=-1)
        qa, ka, va = jnp.split(pa, 3, axis=-1)
        out_a = moba_attention(split_heads(qa, N_GROUP_HEADS), split_heads(ka, N_GROUP_HEADS),
                               split_heads(va, N_GROUP_HEADS), cos_a, sin_a)
        out_b = rwkv7_time_mix(pb, rwkv_mu[l], rwkv_w0[l], rwkv_w2[l], rwkv_a0[l], rwkv_a2[l],
                               rwkv_g2[l], rwkv_kk[l], rwkv_ka[l], rwkv_rk[l], rwkv_ln_g[l], rwkv_ln_b[l])
        uc, vc = jnp.split(pc, 2, axis=-1)
        out_c = sgu_mix(uc, vc, sgu_ln_g[l], sgu_ln_b[l], sgu_w[l], sgu_b[l])
        qd, kd, vd = jnp.split(pd, 3, axis=-1)
        lambda_init = 0.8 - 0.6 * math.exp(-0.3 * l)
        out_d = diff_attention(qd, kd, vd, diff_lq1[l], diff_lk1[l], diff_lq2[l], diff_lk2[l],
                               diff_subln_g[l], lambda_init, cos_d, sin_d)
        mixed = jnp.concatenate([out_a, out_b, out_c, out_d], axis=-1)
        x = x + mixed @ w_out[l]
        x = x + hier_moe(rmsnorm(x, norm2_g[l]), router_group_w[l], router_group_b[l],
                         router_expert_w[l], router_expert_b[l],
                         moe_w_gate[l], moe_w_up[l], moe_w_down[l])
    return rmsnorm(x, final_g)
```

```python
import functools
import math

import numpy as np
import jax
import jax.numpy as jnp
from jax import lax
from jax.experimental import pallas as pl
from jax.experimental.pallas import tpu as pltpu

F32 = jnp.float32
BF16 = jnp.bfloat16

D_MODEL = 1024
HEAD_DIM = 64
N_HEADS = 4
GW = N_HEADS * HEAD_DIM
ROPE_THETA = 500000.0
RMS_EPS = 1e-6

MOBA_BLOCK = 256
MOBA_TOPK = 3
RWKV_LN_EPS = 64e-5
RWKV_CHUNK = 64
RWKV_ROWS = 256
SGU_CHUNK = 128
DIFF_QK_DIM = 32
DIFF_SUBLN_EPS = 1e-5

N_EXPERT_GROUPS = 4
EXPERTS_PER_GROUP = 8
N_EXPERTS = 32
EXPERT_FF = 512
MOE_ROWS = 256

ROW_TILE = 512
VMEM_LIMIT = 56 * 1024 * 1024

NEG_INF = float("-inf")


def _dot(a, b):
    return jnp.dot(a, b, preferred_element_type=F32)


def _dot_nt(a, b):
    return lax.dot_general(a, b, (((1,), (1,)), ((), ())), preferred_element_type=F32)


def _split3(x):
    hi = x.astype(BF16)
    r1 = x - hi.astype(F32)
    mid = r1.astype(BF16)
    lo = (r1 - mid.astype(F32)).astype(BF16)
    return hi, mid, lo


def _segsum(x, bd):
    hi, mid, lo = _split3(x)
    return _dot(hi, bd) + _dot(mid, bd) + _dot(lo, bd)


def _sigmoid(x):
    return 1.0 / (1.0 + jnp.exp(-x))


def _gelu(x):
    return 0.5 * x * (1.0 + lax.erf(x * (1.0 / math.sqrt(2.0))))


def _params(*sem):
    return pltpu.CompilerParams(dimension_semantics=sem, vmem_limit_bytes=VMEM_LIMIT)


def _inproj_kernel(x_ref, g_ref, w_ref, ca_ref, sa_ref, cd_ref, sd_ref,
                   qa_o, ka_o, va_o, pb_o, pc_o, qd_o, kd_o, vd_o):
    x = x_ref[...]
    ms = jnp.mean(x * x, axis=-1, keepdims=True)
    xn = (x * lax.rsqrt(ms + RMS_EPS) * g_ref[...]).astype(BF16)

    def proj(c0, n=1):
        return _dot(xn, w_ref[:, c0 * GW:(c0 + n) * GW])

    ca, sa = ca_ref[...], sa_ref[...]
    qa_o[...] = ((proj(0) * ca + proj(1) * sa) * (HEAD_DIM ** -0.5)).astype(BF16)
    ka_o[...] = (proj(2) * ca + proj(3) * sa).astype(BF16)
    va_o[...] = proj(4).astype(BF16)
    pb_o[...] = proj(5, 4).astype(BF16)
    pc_o[...] = proj(9, 2).astype(BF16)
    cd, sd = cd_ref[...], sd_ref[...]
    qd_o[...] = ((proj(11) * cd + proj(12) * sd) * (DIFF_QK_DIM ** -0.5)).astype(BF16)
    kd_o[...] = (proj(13) * cd + proj(14) * sd).astype(BF16)
    vd_o[...] = proj(15).astype(BF16)


def _inproj(x2, g, w_all, ca, sa, cd, sd, seq):
    t = x2.shape[0]
    tm = min(ROW_TILE, seq)
    per_seq = seq // tm
    row = lambda i: (i, 0)
    tab = lambda i: (i % per_seq, 0)
    const = lambda i: (0, 0)
    out_w = [GW, GW, GW, 4 * GW, 2 * GW, GW, GW, GW]
    return pl.pallas_call(
        _inproj_kernel,
        grid=(t // tm,),
        in_specs=[pl.BlockSpec((tm, D_MODEL), row),
                  pl.BlockSpec((1, D_MODEL), const),
                  pl.BlockSpec((D_MODEL, 16 * GW), const),
                  pl.BlockSpec((tm, GW), tab), pl.BlockSpec((tm, GW), tab),
                  pl.BlockSpec((tm, GW), tab), pl.BlockSpec((tm, GW), tab)],
        out_specs=[pl.BlockSpec((tm, w), row) for w in out_w],
        out_shape=[jax.ShapeDtypeStruct((t, w), BF16) for w in out_w],
        compiler_params=_params("parallel"),
        name="inproj",
    )(x2, g, w_all, ca, sa, cd, sd)


def _flash_update(s, v_blk, m_ref, l_ref, acc_ref, first):
    if first:
        m_new = jnp.max(s, axis=1, keepdims=True)
        p = jnp.exp(s - m_new)
        l_ref[...] = jnp.sum(p, axis=1, keepdims=True)
        acc_ref[...] = _dot(p.astype(BF16), v_blk)
    else:
        m_old = m_ref[...]
        m_new = jnp.maximum(m_old, jnp.max(s, axis=1, keepdims=True))
        alpha = jnp.exp(m_old - m_new)
        p = jnp.exp(s - m_new)
        l_ref[...] = alpha * l_ref[...] + jnp.sum(p, axis=1, keepdims=True)
        acc_ref[...] = alpha * acc_ref[...] + _dot(p.astype(BF16), v_blk)
    m_ref[...] = m_new


def _moba_kernel(q_ref, k_ref, v_ref, avg_ref, o_ref, m_ref, l_ref, acc_ref, oacc_ref, *, nb, n_sel):
    blk = MOBA_BLOCK
    j = pl.program_id(1)
    q = q_ref[0]
    lane = lax.broadcasted_iota(jnp.int32, (1, GW), 1)
    kmean = _dot(avg_ref[...], k_ref[0])
    km_hi = kmean.astype(BF16)
    km_lo = (kmean - km_hi.astype(F32)).astype(BF16)
    row = lax.broadcasted_iota(jnp.int32, (blk, blk), 0)
    col = lax.broadcasted_iota(jnp.int32, (blk, blk), 1)
    causal = row >= col
    nlane = lax.broadcasted_iota(jnp.int32, (blk, 128), 1)
    past = nlane < j
    off_j = pl.multiple_of(j * blk, blk)

    for h in range(N_HEADS):
        hm = (lane // HEAD_DIM) == h
        qm = jnp.where(hm, q, jnp.zeros_like(q))
        gate = _dot_nt(qm, km_hi) + _dot_nt(qm, km_lo)
        gate = jnp.where(past, gate, NEG_INF)
        rank = jnp.zeros((blk, 128), F32)
        for mblk in range(nb):
            gm = gate[:, mblk:mblk + 1]
            beats = jnp.where(gm > gate, 1.0, jnp.where((gm == gate) & (nlane > mblk), 1.0, 0.0))
            rank = rank + beats
        sel = jnp.where(past & (rank < n_sel), 1.0, 0.0)

        s = _dot_nt(qm, k_ref[0, pl.ds(off_j, blk), :])
        s = jnp.where(causal, s, NEG_INF)
        _flash_update(s, v_ref[0, pl.ds(off_j, blk), :], m_ref, l_ref, acc_ref, True)

        def body(n, carry):
            off = pl.multiple_of(n * blk, blk)
            sn = _dot_nt(qm, k_ref[0, pl.ds(off, blk), :])
            selcol = jnp.sum(jnp.where(nlane == n, sel, 0.0), axis=1, keepdims=True) > 0.5
            sn = jnp.where(selcol, sn, NEG_INF)
            _flash_update(sn, v_ref[0, pl.ds(off, blk), :], m_ref, l_ref, acc_ref, False)
            return carry

        lax.fori_loop(0, j, body, 0)
        o = jnp.where(hm, acc_ref[...] / l_ref[...], 0.0)
        if h == 0:
            oacc_ref[...] = o
        else:
            oacc_ref[...] += o
    o_ref[0] = oacc_ref[...].astype(BF16)


def _moba(q, k, v):
    b, s, _ = q.shape
    blk = MOBA_BLOCK
    nb = s // blk
    n_sel = min(MOBA_TOPK, max(nb - 1, 1))
    avg = jnp.where(jnp.arange(s)[None, :] // blk == jnp.arange(128)[:, None], 1.0 / blk, 0.0).astype(BF16)
    return pl.pallas_call(
        functools.partial(_moba_kernel, nb=nb, n_sel=n_sel),
        grid=(b, nb),
        in_specs=[pl.BlockSpec((1, blk, GW), lambda i, j: (i, j, 0)),
                  pl.BlockSpec((1, s, GW), lambda i, j: (i, 0, 0)),
                  pl.BlockSpec((1, s, GW), lambda i, j: (i, 0, 0)),
                  pl.BlockSpec((128, s), lambda i, j: (0, 0))],
        out_specs=pl.BlockSpec((1, blk, GW), lambda i, j: (i, j, 0)),
        out_shape=jax.ShapeDtypeStruct((b, s, GW), BF16),
        scratch_shapes=[pltpu.VMEM((blk, 1), F32), pltpu.VMEM((blk, 1), F32),
                        pltpu.VMEM((blk, GW), F32), pltpu.VMEM((blk, GW), F32)],
        compiler_params=_params("parallel", "arbitrary"),
        name="moba",
    )(q, k, v, avg)


def _diff_kernel(q_ref, k_ref, v_ref, lqk_ref, g_ref, o_ref, m_ref, l_ref, acc_ref, oacc_ref, *, lambda_init):
    blk = MOBA_BLOCK
    j = pl.program_id(1)
    q = q_ref[0]
    lane = lax.broadcasted_iota(jnp.int32, (1, GW), 1)
    row = lax.broadcasted_iota(jnp.int32, (blk, blk), 0)
    col = lax.broadcasted_iota(jnp.int32, (blk, blk), 1)
    causal = row >= col
    off_j = pl.multiple_of(j * blk, blk)
    lqk = lqk_ref[...]
    lam = (jnp.exp(jnp.sum(lqk[0:1] * lqk[1:2], axis=1, keepdims=True))
           - jnp.exp(jnp.sum(lqk[2:3] * lqk[3:4], axis=1, keepdims=True)) + lambda_init)

    for h in range(N_HEADS):
        hm = (lane // HEAD_DIM) == h
        for i in range(2):
            mm = (lane // DIFF_QK_DIM) == (2 * h + i)
            qm = jnp.where(mm, q, jnp.zeros_like(q))
            s = _dot_nt(qm, k_ref[0, pl.ds(off_j, blk), :])
            s = jnp.where(causal, s, NEG_INF)
            _flash_update(s, v_ref[0, pl.ds(off_j, blk), :], m_ref, l_ref, acc_ref, True)

            def body(n, carry):
                off = pl.multiple_of(n * blk, blk)
                sn = _dot_nt(qm, k_ref[0, pl.ds(off, blk), :])
                _flash_update(sn, v_ref[0, pl.ds(off, blk), :], m_ref, l_ref, acc_ref, False)
                return carry

            lax.fori_loop(0, j, body, 0)
            o = jnp.where(hm, acc_ref[...] / l_ref[...], 0.0)
            if h == 0 and i == 0:
                oacc_ref[...] = o
            elif i == 0:
                oacc_ref[...] += o
            else:
                oacc_ref[...] -= lam * o
    x = oacc_ref[...]
    x2 = x * x
    ms = jnp.zeros_like(x)
    for h in range(N_HEADS):
        hm = (lane // HEAD_DIM) == h
        ms_h = jnp.sum(jnp.where(hm, x2, 0.0), axis=1, keepdims=True) * (1.0 / HEAD_DIM)
        ms = ms + jnp.where(hm, ms_h, 0.0)
    y = x * lax.rsqrt(ms + DIFF_SUBLN_EPS) * g_ref[...] * (1.0 - lambda_init)
    o_ref[0] = y.astype(BF16)


def _diff(q, k, v, lqk, g_tiled, lambda_init):
    b, s, _ = q.shape
    blk = MOBA_BLOCK
    nb = s // blk
    return pl.pallas_call(
        functools.partial(_diff_kernel, lambda_init=lambda_init),
        grid=(b, nb),
        in_specs=[pl.BlockSpec((1, blk, GW), lambda i, j: (i, j, 0)),
                  pl.BlockSpec((1, s, GW), lambda i, j: (i, 0, 0)),
                  pl.BlockSpec((1, s, GW), lambda i, j: (i, 0, 0)),
                  pl.BlockSpec((4, 128), lambda i, j: (0, 0)),
                  pl.BlockSpec((1, GW), lambda i, j: (0, 0))],
        out_specs=pl.BlockSpec((1, blk, GW), lambda i, j: (i, j, 0)),
        out_shape=jax.ShapeDtypeStruct((b, s, GW), BF16),
        scratch_shapes=[pltpu.VMEM((blk, 1), F32), pltpu.VMEM((blk, 1), F32),
                        pltpu.VMEM((blk, GW), F32), pltpu.VMEM((blk, GW), F32)],
        compiler_params=_params("parallel", "arbitrary"),
        name="diffattn",
    )(q, k, v, lqk, g_tiled)


def _sgu_kernel(pc_ref, lng_ref, lnb_ref, w_ref, bias_ref, o_ref):
    rows = pc_ref.shape[0]
    pc = pc_ref[...].astype(F32)
    u = _gelu(pc[:, :GW])
    v = _gelu(pc[:, GW:])
    mu = jnp.mean(v, axis=-1, keepdims=True)
    d = v - mu
    var = jnp.mean(d * d, axis=-1, keepdims=True)
    vn = d * lax.rsqrt(var + 1e-5) * lng_ref[...] + lnb_ref[...]
    lane = lax.broadcasted_iota(jnp.int32, (1, GW), 1)
    tr = lax.broadcasted_iota(jnp.int32, (SGU_CHUNK, SGU_CHUNK), 0)
    tc = lax.broadcasted_iota(jnp.int32, (SGU_CHUNK, SGU_CHUNK), 1)
    wm = [jnp.where(tr >= tc, w_ref[h], 0.0).astype(BF16) for h in range(N_HEADS)]
    for c in range(rows // SGU_CHUNK):
        sl = slice(c * SGU_CHUNK, (c + 1) * SGU_CHUNK)
        vc = vn[sl]
        mixed = bias_ref[...]
        for h in range(N_HEADS):
            vm = jnp.where((lane // HEAD_DIM) == h, vc, 0.0).astype(BF16)
            mixed = mixed + _dot(wm[h], vm)
        o_ref[sl, :] = (u[sl] * mixed).astype(BF16)


def _sgu(pc, lng, lnb, w, bias2d):
    t = pc.shape[0]
    tm = ROW_TILE
    const2 = lambda i: (0, 0)
    return pl.pallas_call(
        _sgu_kernel,
        grid=(t // tm,),
        in_specs=[pl.BlockSpec((tm, 2 * GW), lambda i: (i, 0)),
                  pl.BlockSpec((1, GW), const2), pl.BlockSpec((1, GW), const2),
                  pl.BlockSpec((N_HEADS, SGU_CHUNK, SGU_CHUNK), lambda i: (0, 0, 0)),
                  pl.BlockSpec((SGU_CHUNK, GW), const2)],
        out_specs=pl.BlockSpec((tm, GW), lambda i: (i, 0)),
        out_shape=jax.ShapeDtypeStruct((t, GW), BF16),
        compiler_params=_params("parallel"),
        name="sgu",
    )(pc, lng, lnb, w, bias2d)


def _rwkv_kernel(pb_ref, mu_ref, wl_ref, w0_ref, a0_ref, kk_ref, ka_ref, rk_ref, lng_ref, lnb_ref,
                 bd_ref, ltri_ref, o_ref,
                 state_ref, prev_ref, r_s, lw_s, k_s, v_s, al_s, be_s, y_s):
    rt = pb_ref.shape[1]
    c_len = RWKV_CHUNK

    @pl.when(pl.program_id(1) == 0)
    def _():
        state_ref[...] = jnp.zeros_like(state_ref)
        prev_ref[...] = jnp.zeros_like(prev_ref)

    bd = bd_ref[...]
    xt = pb_ref[0].astype(F32)
    rowid = lax.broadcasted_iota(jnp.int32, (rt, 1), 0)
    prev = jnp.where(rowid == 0, prev_ref[...], pltpu.roll(xt, 1, axis=0))
    prev_ref[...] = xt[rt - 1:rt, :]
    xs = xt + (prev - xt) * mu_ref[...]
    r = xs[:, 0:GW]
    k = xs[:, GW:2 * GW]
    v = xs[:, 2 * GW:3 * GW]
    lo = xs[:, 3 * GW:4 * GW]
    lane = lax.broadcasted_iota(jnp.int32, (1, GW), 1)
    lin = jnp.where(lane < 64, jnp.tanh(lo), jnp.where(lane < 128, lo, _sigmoid(lo)))
    z = _dot(lin.astype(BF16), wl_ref[...])
    wz = -(w0_ref[...] + z[:, 0:GW])
    softplus = jnp.maximum(wz, 0.0) + jnp.log(1.0 + jnp.exp(-jnp.abs(wz)))
    lw = -jnp.exp(-softplus - 0.5)
    a = _sigmoid(a0_ref[...] + z[:, GW:2 * GW])
    gate = z[:, 2 * GW:3 * GW]
    kk = k * kk_ref[...]
    kk = kk / jnp.maximum(jnp.sqrt(_segsum(kk * kk, bd)), 1e-12)
    k2 = k * (1.0 + (a - 1.0) * ka_ref[...])
    bonus = _segsum(r * k2 * rk_ref[...], bd) * v

    r_s[...] = r
    lw_s[...] = lw
    k_s[...] = k2
    v_s[...] = v
    al_s[...] = -kk
    be_s[...] = kk * a

    hms = [(lane // HEAD_DIM) == h for h in range(N_HEADS)]
    brow = lax.broadcasted_iota(jnp.int32, (GW, GW), 0)
    bcol = lax.broadcasted_iota(jnp.int32, (GW, GW), 1)
    strict = brow > bcol
    lower = brow >= bcol
    eye = brow == bcol
    ltri = ltri_ref[...]

    def big(x):
        return jnp.concatenate([jnp.where(hm, x, 0.0) for hm in hms], axis=0)

    def unbig(x):
        return x[0:c_len] + x[c_len:2 * c_len] + x[2 * c_len:3 * c_len] + x[3 * c_len:4 * c_len]

    def chunk(c, carry):
        sl = pl.ds(pl.multiple_of(c * c_len, c_len), c_len)
        lwc = lw_s[sl, :]
        h3 = _split3(lwc)
        cs = _dot(ltri, h3[0]) + _dot(ltri, h3[1]) + _dot(ltri, h3[2])
        cs_end = cs[c_len - 1:c_len, :]
        e_neg = jnp.exp(-cs)
        e_rem = jnp.exp(cs_end - cs)
        al, be, kc, rc, vc = al_s[sl, :], be_s[sl, :], k_s[sl, :], r_s[sl, :], v_s[sl, :]
        r_t = rc * jnp.exp(cs)
        a_b = big(al * jnp.exp(cs - lwc)).astype(BF16)
        r_b = big(r_t).astype(BF16)
        bi_b = big(be * e_neg).astype(BF16)
        ki_b = big(kc * e_neg).astype(BF16)
        v_b = big(vc).astype(BF16)
        a_ab = jnp.where(strict, _dot_nt(a_b, bi_b), 0.0)
        a_ak = jnp.where(strict, _dot_nt(a_b, ki_b), 0.0)
        a_rb = jnp.where(lower, _dot_nt(r_b, bi_b), 0.0).astype(BF16)
        a_rk = jnp.where(lower, _dot_nt(r_b, ki_b), 0.0).astype(BF16)
        p = a_ab
        tinv = jnp.where(eye, 1.0, a_ab)
        for _ in range(5):
            pb = p.astype(BF16)
            p = _dot(pb, pb)
            tinv = tinv + _dot(tinv.astype(BF16), p.astype(BF16))
        tb = tinv.astype(BF16)
        av = _dot(a_ak.astype(BF16), v_b)
        p1 = _dot(tb, a_b).astype(BF16)
        p2 = _dot(tb, av.astype(BF16)).astype(BF16)
        q_tok = r_t + unbig(_dot(a_rb, p1))
        z_tok = unbig(_dot(a_rb, p2) + _dot(a_rk, v_b))
        bd_t = big(be * e_rem).T.astype(BF16)
        kd_t = big(kc * e_rem).T.astype(BF16)
        m_c = jnp.where(eye, jnp.exp(cs_end), 0.0) + _dot(bd_t, p1)
        n_c = _dot(bd_t, p2) + _dot(kd_t, v_b)
        skb = state_ref[...].astype(BF16)
        y_s[sl, :] = _dot(q_tok.astype(BF16), skb) + z_tok
        state_ref[...] = _dot(m_c.astype(BF16), skb) + n_c
        return carry

    lax.fori_loop(0, rt // c_len, chunk, 0)

    y = y_s[...]
    mean = _segsum(y, bd) * (1.0 / HEAD_DIM)
    d = y - mean
    var = _segsum(d * d, bd) * (1.0 / HEAD_DIM)
    yn = d * lax.rsqrt(var + RWKV_LN_EPS) * lng_ref[...] + lnb_ref[...]
    o_ref[0] = ((yn + bonus) * gate).astype(BF16)


def _rwkv(pb, mu, wl, w0, a0, kk, ka, rk, lng, lnb, bd, ltri):
    b, s, _ = pb.shape
    rt = RWKV_ROWS
    c2 = lambda i, j: (0, 0)
    vec = pl.BlockSpec((1, GW), c2)
    return pl.pallas_call(
        _rwkv_kernel,
        grid=(b, s // rt),
        in_specs=[pl.BlockSpec((1, rt, 4 * GW), lambda i, j: (i, j, 0)),
                  pl.BlockSpec((1, 4 * GW), c2),
                  pl.BlockSpec((GW, 3 * GW), c2),
                  vec, vec, vec, vec, vec, vec, vec,
                  pl.BlockSpec((GW, GW), c2),
                  pl.BlockSpec((RWKV_CHUNK, RWKV_CHUNK), c2)],
        out_specs=pl.BlockSpec((1, rt, GW), lambda i, j: (i, j, 0)),
        out_shape=jax.ShapeDtypeStruct((b, s, GW), BF16),
        scratch_shapes=[pltpu.VMEM((GW, GW), F32), pltpu.VMEM((1, 4 * GW), F32)]
                       + [pltpu.VMEM((rt, GW), F32) for _ in range(7)],
        compiler_params=_params("parallel", "arbitrary"),
        name="rwkv7",
    )(pb, mu, wl, w0, a0, kk, ka, rk, lng, lnb, bd, ltri)


def _outproj_kernel(oa_ref, ob_ref, oc_ref, od_ref, x_ref, w_ref, g_ref, wrh_ref, wrl_ref, br_ref,
                    h_o, hn_o, eid_o, wts_o):
    mixed = jnp.concatenate([oa_ref[...], ob_ref[...], oc_ref[...], od_ref[...]], axis=1)
    h = x_ref[...] + _dot(mixed, w_ref[...])
    h_o[...] = h
    ms = jnp.mean(h * h, axis=-1, keepdims=True)
    hn = h * lax.rsqrt(ms + RMS_EPS) * g_ref[...]
    hn_o[...] = hn
    hi = hn.astype(BF16)
    lo = (hn - hi.astype(F32)).astype(BF16)
    wrh = wrh_ref[...]
    logits = _dot(hi, wrh) + _dot(lo, wrh) + _dot(hi, wrl_ref[...]) + br_ref[...]
    tm = logits.shape[0]
    lane = lax.broadcasted_iota(jnp.int32, (tm, 128), 1)
    lanef = lane.astype(F32)
    big_idx = 1e9

    def rmax(t):
        return jnp.max(t, axis=1, keepdims=True)

    def rmin(t):
        return jnp.min(t, axis=1, keepdims=True)

    def rsum(t):
        return jnp.sum(t, axis=1, keepdims=True)

    gmask = lane < N_EXPERT_GROUPS
    gl = jnp.where(gmask, logits, NEG_INF)
    gmax = rmax(gl)
    gidx = rmin(jnp.where(gl == gmax, lanef, big_idx))
    gp = 1.0 / rsum(jnp.where(gmask, jnp.exp(logits - gmax), 0.0))
    e_lo = N_EXPERT_GROUPS + gidx * EXPERTS_PER_GROUP
    emask = (lanef >= e_lo) & (lanef < e_lo + EXPERTS_PER_GROUP)
    el = jnp.where(emask, logits, NEG_INF)
    emax = rmax(el)
    esum = rsum(jnp.where(emask, jnp.exp(logits - emax), 0.0))
    i1 = rmin(jnp.where(el == emax, lanef, big_idx))
    el2 = jnp.where(lanef == i1, NEG_INF, el)
    m2 = rmax(el2)
    i2 = rmin(jnp.where(el2 == m2, lanef, big_idx))
    p1 = 1.0 / esum
    p2 = jnp.exp(m2 - emax) / esum
    w1 = gp * (p1 / (p1 + p2))
    w2 = gp * (p2 / (p1 + p2))
    eid = jnp.where(lane == 0, i1 - N_EXPERT_GROUPS, jnp.where(lane == 1, i2 - N_EXPERT_GROUPS, 0.0))
    eid_o[...] = eid.astype(jnp.int32)
    wts_o[...] = jnp.where(lane == 0, w1, jnp.where(lane == 1, w2, 0.0))


def _outproj(oa, ob, oc, od, x2, w_out, g2, wr_hi, wr_lo, br):
    t = x2.shape[0]
    tm = ROW_TILE
    row = lambda i: (i, 0)
    const = lambda i: (0, 0)
    mix = pl.BlockSpec((tm, GW), row)
    return pl.pallas_call(
        _outproj_kernel,
        grid=(t // tm,),
        in_specs=[mix, mix, mix, mix,
                  pl.BlockSpec((tm, D_MODEL), row),
                  pl.BlockSpec((D_MODEL, D_MODEL), const),
                  pl.BlockSpec((1, D_MODEL), const),
                  pl.BlockSpec((D_MODEL, 128), const), pl.BlockSpec((D_MODEL, 128), const),
                  pl.BlockSpec((1, 128), const)],
        out_specs=[pl.BlockSpec((tm, D_MODEL), row), pl.BlockSpec((tm, D_MODEL), row),
                   pl.BlockSpec((tm, 128), row), pl.BlockSpec((tm, 128), row)],
        out_shape=[jax.ShapeDtypeStruct((t, D_MODEL), F32), jax.ShapeDtypeStruct((t, D_MODEL), F32),
                   jax.ShapeDtypeStruct((t, 128), jnp.int32), jax.ShapeDtypeStruct((t, 128), F32)],
        compiler_params=_params("parallel"),
        name="outproj_router",
    )(oa, ob, oc, od, x2, w_out, g2, wr_hi, wr_lo, br)


def _expert_kernel(blk_e_ref, blk_nv_ref, tok_ref, dst_ref, hn_hbm, wg_ref, wu_ref, wd_ref, y_hbm,
                   xbuf, ybuf, gsem, ssem):
    i = pl.program_id(0)
    nv = blk_nv_ref[i]

    @pl.when(i == 0)
    def _():
        xbuf[...] = jnp.zeros_like(xbuf)

    def row_in(r):
        return pltpu.make_async_copy(hn_hbm.at[pl.ds(tok_ref[0, 0, r], 1)], xbuf.at[pl.ds(r, 1)], gsem)

    def row_out(r):
        return pltpu.make_async_copy(ybuf.at[pl.ds(r, 1)], y_hbm.at[pl.ds(dst_ref[0, 0, r], 1)], ssem)

    def each(fn):
        def body(r, carry):
            fn(r)
            return carry
        lax.fori_loop(0, nv, body, 0)

    @pl.when(nv > 0)
    def _():
        each(lambda r: row_in(r).start())
        each(lambda r: row_in(r).wait())
        xb = xbuf[...].astype(BF16)
        hg = _dot(xb, wg_ref[0])
        hu = _dot(xb, wu_ref[0])
        hdn = (hg * _sigmoid(hg) * hu).astype(BF16)
        ybuf[...] = _dot(hdn, wd_ref[0])
        each(lambda r: row_out(r).start())
        each(lambda r: row_out(r).wait())


def _experts(blk_e, blk_nv, row_tok, row_dst, hn, wg, wu, wd, m_rows):
    nblk = blk_e.shape[0]
    r = MOE_ROWS
    grid_spec = pltpu.PrefetchScalarGridSpec(
        num_scalar_prefetch=2,
        grid=(nblk,),
        in_specs=[pl.BlockSpec((1, 1, r), lambda i, be, nv: (i, 0, 0), memory_space=pltpu.SMEM),
                  pl.BlockSpec((1, 1, r), lambda i, be, nv: (i, 0, 0), memory_space=pltpu.SMEM),
                  pl.BlockSpec(memory_space=pl.ANY),
                  pl.BlockSpec((1, D_MODEL, EXPERT_FF), lambda i, be, nv: (be[i], 0, 0)),
                  pl.BlockSpec((1, D_MODEL, EXPERT_FF), lambda i, be, nv: (be[i], 0, 0)),
                  pl.BlockSpec((1, EXPERT_FF, D_MODEL), lambda i, be, nv: (be[i], 0, 0))],
        out_specs=pl.BlockSpec(memory_space=pl.ANY),
        scratch_shapes=[pltpu.VMEM((r, D_MODEL), F32), pltpu.VMEM((r, D_MODEL), F32),
                        pltpu.SemaphoreType.DMA(()), pltpu.SemaphoreType.DMA(())],
    )
    return pl.pallas_call(
        _expert_kernel,
        grid_spec=grid_spec,
        out_shape=jax.ShapeDtypeStruct((m_rows, D_MODEL), F32),
        compiler_params=_params("arbitrary"),
        name="experts",
    )(blk_e, blk_nv, row_tok, row_dst, hn, wg, wu, wd)


def _dispatch(eid, t):
    r = MOE_ROWS
    m = 2 * t
    e_flat = eid[:, :2].reshape(m)
    order = jnp.argsort(e_flat, stable=True).astype(jnp.int32)
    counts = jnp.sum((e_flat[:, None] == jnp.arange(N_EXPERTS, dtype=jnp.int32)[None, :]).astype(jnp.int32), axis=0)
    starts = jnp.cumsum(counts) - counts
    pcounts = (counts + r - 1) // r * r
    pends = jnp.cumsum(pcounts)
    pstarts = pends - pcounts
    nblk = m // r + N_EXPERTS
    blk_e = jnp.minimum(jnp.searchsorted(pends, jnp.arange(nblk, dtype=jnp.int32) * r, side="right"),
                        N_EXPERTS - 1).astype(jnp.int32)
    p = jnp.arange(nblk * r, dtype=jnp.int32)
    e_p = blk_e[p // r]
    off = p - pstarts[e_p]
    valid = off < counts[e_p]
    a = order[jnp.clip(starts[e_p] + off, 0, m - 1)]
    row_tok = jnp.where(valid, a // 2, 0).astype(jnp.int32)
    row_dst = jnp.where(valid, (a % 2) * t + a // 2, 0).astype(jnp.int32)
    blk_nv = jnp.sum(valid.reshape(nblk, r).astype(jnp.int32), axis=1)
    return blk_e, blk_nv, row_tok.reshape(nblk, 1, r), row_dst.reshape(nblk, 1, r)


def _combine_kernel(h_ref, y0_ref, y1_ref, wts_ref, fg_ref, o_ref, *, final):
    w = wts_ref[...]
    x = h_ref[...] + w[:, 0:1] * y0_ref[0] + w[:, 1:2] * y1_ref[0]
    if final:
        ms = jnp.mean(x * x, axis=-1, keepdims=True)
        x = x * lax.rsqrt(ms + RMS_EPS) * fg_ref[...]
    o_ref[...] = x


def _combine(h, y, wts, fg, final):
    t = h.shape[0]
    tm = ROW_TILE
    row = lambda i: (i, 0)
    return pl.pallas_call(
        functools.partial(_combine_kernel, final=final),
        grid=(t // tm,),
        in_specs=[pl.BlockSpec((tm, D_MODEL), row),
                  pl.BlockSpec((1, tm, D_MODEL), lambda i: (0, i, 0)),
                  pl.BlockSpec((1, tm, D_MODEL), lambda i: (1, i, 0)),
                  pl.BlockSpec((tm, 128), row),
                  pl.BlockSpec((1, D_MODEL), lambda i: (0, 0))],
        out_specs=pl.BlockSpec((tm, D_MODEL), row),
        out_shape=jax.ShapeDtypeStruct((t, D_MODEL), F32),
        compiler_params=_params("parallel"),
        name="combine",
    )(h, y, y, wts, fg)


def _rope_tables(seq, head_width, rot_dim):
    half = rot_dim // 2
    inv = ROPE_THETA ** (-jnp.arange(0, rot_dim, 2, dtype=F32) / rot_dim)
    ang = jnp.arange(seq, dtype=F32)[:, None] * inv[None, :]
    cos, sin = jnp.cos(ang), jnp.sin(ang)
    d = np.arange(GW) % head_width
    idx = d % half
    cos_t = jnp.where((d < rot_dim)[None, :], cos[:, idx], 1.0)
    sin_t = jnp.where((d < half)[None, :], -sin[:, idx], jnp.where((d < rot_dim)[None, :], sin[:, idx], 0.0))
    return cos_t.astype(F32), sin_t.astype(F32)


def _swap_cols(w, head_width, rot_dim):
    half = rot_dim // 2
    c = np.arange(GW)
    d = c % head_width
    partner = np.where(d < half, c + half, np.where(d < rot_dim, c - half, c))
    keep = (d < rot_dim).astype(np.float32)
    return w[:, partner] * keep[None, :]


def _layer_weights(l, w_in, rwkv_mu, rwkv_w2, rwkv_a2, rwkv_g2, sgu_w, sgu_b, diff_l, diff_subln_g,
                   router_group_w, router_group_b, router_expert_w, router_expert_b):
    wi = w_in[l]
    a0, b0, c0, d0 = 0, 3 * GW, 3 * GW + 4 * GW, 3 * GW + 4 * GW + 2 * GW
    qa, ka, va = wi[:, a0:a0 + GW], wi[:, a0 + GW:a0 + 2 * GW], wi[:, a0 + 2 * GW:a0 + 3 * GW]
    bsl = [(0, 256), (320, 576), (576, 832), (256, 320), (832, 896), (896, 1024)]
    wb = jnp.concatenate([wi[:, b0 + s:b0 + e] for s, e in bsl], axis=1)
    mu = jnp.concatenate([rwkv_mu[l][s:e] for s, e in bsl])[None, :]
    wc = wi[:, c0:c0 + 2 * GW]
    qd, kd, vd = wi[:, d0:d0 + GW], wi[:, d0 + GW:d0 + 2 * GW], wi[:, d0 + 2 * GW:d0 + 3 * GW]
    w_all = jnp.concatenate(
        [qa, _swap_cols(qa, HEAD_DIM, 16), ka, _swap_cols(ka, HEAD_DIM, 16), va, wb, wc,
         qd, _swap_cols(qd, DIFF_QK_DIM, 8), kd, _swap_cols(kd, DIFF_QK_DIM, 8), vd], axis=1).astype(BF16)
    wl = jnp.zeros((GW, 3 * GW), F32)
    wl = wl.at[0:64, 0:GW].set(rwkv_w2[l]).at[64:128, GW:2 * GW].set(rwkv_a2[l]).at[128:256, 2 * GW:].set(rwkv_g2[l])
    bias2d = jnp.repeat(sgu_b[l].T, HEAD_DIM, axis=1)
    lqk = jnp.zeros((4, 128), F32).at[:, :DIFF_QK_DIM].set(jnp.stack([d[l] for d in diff_l]))
    wr = jnp.zeros((D_MODEL, 128), F32).at[:, :N_EXPERT_GROUPS].set(router_group_w[l])
    wr = wr.at[:, N_EXPERT_GROUPS:N_EXPERT_GROUPS + N_EXPERTS].set(router_expert_w[l])
    wr_hi = wr.astype(BF16)
    wr_lo = (wr - wr_hi.astype(F32)).astype(BF16)
    br = jnp.zeros((1, 128), F32).at[0, :N_EXPERT_GROUPS].set(router_group_b[l])
    br = br.at[0, N_EXPERT_GROUPS:N_EXPERT_GROUPS + N_EXPERTS].set(router_expert_b[l])
    return dict(w_all=w_all, mu=mu, wl=wl.astype(BF16), bias2d=bias2d, lqk=lqk,
                subln=jnp.tile(diff_subln_g[l], N_HEADS)[None, :], wr_hi=wr_hi, wr_lo=wr_lo, br=br)


def kernel(x, norm1_g, w_in, rwkv_mu, rwkv_w0, rwkv_w2, rwkv_a0, rwkv_a2, rwkv_g2, rwkv_kk, rwkv_ka, rwkv_rk, rwkv_ln_g, rwkv_ln_b, sgu_ln_g, sgu_ln_b, sgu_w, sgu_b, diff_lq1, diff_lk1, diff_lq2, diff_lk2, diff_subln_g, w_out, norm2_g, router_group_w, router_group_b, router_expert_w, router_expert_b, moe_w_gate, moe_w_up, moe_w_down, final_g):
    b, s, d = x.shape
    depth = w_in.shape[0]
    t = b * s
    ca, sa = _rope_tables(s, HEAD_DIM, 16)
    cd, sd = _rope_tables(s, DIFF_QK_DIM, 8)
    seg = np.arange(GW) // HEAD_DIM
    bd = jnp.asarray(seg[:, None] == seg[None, :], BF16)
    ltri = jnp.asarray(np.tril(np.ones((RWKV_CHUNK, RWKV_CHUNK))), BF16)
    row = lambda p: p[None, :]
    x2 = x.reshape(t, d)
    for l in range(depth):
        lw = _layer_weights(l, w_in, rwkv_mu, rwkv_w2, rwkv_a2, rwkv_g2, sgu_w, sgu_b,
                            (diff_lq1, diff_lk1, diff_lq2, diff_lk2), diff_subln_g,
                            router_group_w, router_group_b, router_expert_w, router_expert_b)
        qa, ka, va, pb, pc, qd, kd, vd = _inproj(x2, row(norm1_g[l]), lw["w_all"], ca, sa, cd, sd, s)
        seq3 = lambda z: z.reshape(b, s, z.shape[-1])
        out_a = _moba(seq3(qa), seq3(ka), seq3(va))
        out_b = _rwkv(seq3(pb), lw["mu"], lw["wl"], row(rwkv_w0[l]), row(rwkv_a0[l]), row(rwkv_kk[l]),
                      row(rwkv_ka[l]), row(rwkv_rk[l].reshape(GW)), row(rwkv_ln_g[l]), row(rwkv_ln_b[l]), bd, ltri)
        out_c = _sgu(pc, row(sgu_ln_g[l]), row(sgu_ln_b[l]), sgu_w[l], lw["bias2d"])
        lambda_init = 0.8 - 0.6 * math.exp(-0.3 * l)
        out_d = _diff(seq3(qd), seq3(kd), seq3(vd), lw["lqk"], lw["subln"], lambda_init)
        h, hn, eid, wts = _outproj(out_a.reshape(t, GW), out_b.reshape(t, GW), out_c, out_d.reshape(t, GW),
                                   x2, w_out[l].astype(BF16), row(norm2_g[l]), lw["wr_hi"], lw["wr_lo"], lw["br"])
        blk_e, blk_nv, row_tok, row_dst = _dispatch(eid, t)
        y = _experts(blk_e, blk_nv, row_tok, row_dst, hn, moe_w_gate[l].astype(BF16),
                     moe_w_up[l].astype(BF16), moe_w_down[l].astype(BF16), 2 * t)
        x2 = _combine(h, y.reshape(2, t, d), wts, row(final_g), l == depth - 1)
    return x2.reshape(b, s, d)
```

```python
import functools
import math

import numpy as np
import jax
import jax.numpy as jnp
from jax import lax
from jax.experimental import pallas as pl
from jax.experimental.pallas import tpu as pltpu

F32 = jnp.float32
BF16 = jnp.bfloat16

D_MODEL = 1024
HEAD_DIM = 64
N_HEADS = 4
GW = N_HEADS * HEAD_DIM
ROPE_THETA = 500000.0
RMS_EPS = 1e-6

ATT_BLOCK = 256
ATT_SUB = 256
LOG2E = 1.4426950408889634
MOBA_TOPK = 3
RWKV_LN_EPS = 64e-5
RWKV_CHUNK = 64
RWKV_ROWS = 256
SGU_CHUNK = 128
DIFF_QK_DIM = 32
DIFF_SUBLN_EPS = 1e-5

N_EXPERT_GROUPS = 4
EXPERTS_PER_GROUP = 8
N_EXPERTS = 32
EXPERT_FF = 512
MOE_ROWS = 256

ROW_TILE = 512
VMEM_LIMIT = 56 * 1024 * 1024

NEG_INF = float("-inf")


def _dot(a, b):
    return jnp.dot(a, b, preferred_element_type=F32)


def _dot_nt(a, b):
    return lax.dot_general(a, b, (((1,), (1,)), ((), ())), preferred_element_type=F32)


def _split3(x):
    hi = x.astype(BF16)
    r1 = x - hi.astype(F32)
    mid = r1.astype(BF16)
    lo = (r1 - mid.astype(F32)).astype(BF16)
    return hi, mid, lo


def _segsum(x, bd):
    hi, mid, lo = _split3(x)
    return _dot(hi, bd) + _dot(mid, bd) + _dot(lo, bd)


def _sigmoid(x):
    return 1.0 / (1.0 + jnp.exp(-x))


def _gelu(x):
    return 0.5 * x * (1.0 + lax.erf(x * (1.0 / math.sqrt(2.0))))


def _params(*sem):
    return pltpu.CompilerParams(dimension_semantics=sem, vmem_limit_bytes=VMEM_LIMIT)


def _inproj_kernel(x_ref, g_ref, w_ref, wvt_ref, ca_ref, sa_ref, cd_ref, sd_ref,
                   qa_o, ka_o, vta_o, pb_o, pc_o, qd_o, kd_o, vtd_o):
    x = x_ref[...]
    ms = jnp.mean(x * x, axis=-1, keepdims=True)
    xn = (x * lax.rsqrt(ms + RMS_EPS) * g_ref[...]).astype(BF16)

    def proj(c0, n=1):
        return _dot(xn, w_ref[:, c0 * GW:(c0 + n) * GW])

    ca, sa = ca_ref[...], sa_ref[...]
    qa_o[...] = ((proj(0) * ca + proj(1) * sa) * (HEAD_DIM ** -0.5 * LOG2E)).astype(BF16)
    ka_o[...] = (proj(2) * ca + proj(3) * sa).astype(BF16)
    pb_o[...] = proj(4, 4).astype(BF16)
    pc_o[...] = proj(8, 2).astype(BF16)
    cd, sd = cd_ref[...], sd_ref[...]
    qd_o[...] = ((proj(10) * cd + proj(11) * sd) * (DIFF_QK_DIM ** -0.5 * LOG2E)).astype(BF16)
    kd_o[...] = (proj(12) * cd + proj(13) * sd).astype(BF16)
    vt = _dot_nt(wvt_ref[...], xn)
    for kb in range(x.shape[0] // ATT_BLOCK):
        cols = slice(kb * ATT_BLOCK, (kb + 1) * ATT_BLOCK)
        vta_o[kb] = vt[0:GW, cols].astype(BF16)
        vtd_o[kb] = vt[GW:2 * GW, cols].astype(BF16)


def _inproj(x2, g, w_all, wvt, ca, sa, cd, sd, seq):
    t = x2.shape[0]
    tm = min(ROW_TILE, seq)
    per_seq = seq // tm
    kbs = tm // ATT_BLOCK
    row = lambda i: (i, 0)
    tab = lambda i: (i % per_seq, 0)
    const = lambda i: (0, 0)
    vt_spec = pl.BlockSpec((kbs, GW, ATT_BLOCK), lambda i: (i, 0, 0))
    vt_shape = jax.ShapeDtypeStruct((t // ATT_BLOCK, GW, ATT_BLOCK), BF16)
    rows = lambda w: (pl.BlockSpec((tm, w), row), jax.ShapeDtypeStruct((t, w), BF16))
    outs = [rows(GW), rows(GW), (vt_spec, vt_shape), rows(4 * GW), rows(2 * GW), rows(GW), rows(GW),
            (vt_spec, vt_shape)]
    return pl.pallas_call(
        _inproj_kernel,
        grid=(t // tm,),
        in_specs=[pl.BlockSpec((tm, D_MODEL), row),
                  pl.BlockSpec((1, D_MODEL), const),
                  pl.BlockSpec((D_MODEL, 14 * GW), const),
                  pl.BlockSpec((2 * GW, D_MODEL), const),
                  pl.BlockSpec((tm, GW), tab), pl.BlockSpec((tm, GW), tab),
                  pl.BlockSpec((tm, GW), tab), pl.BlockSpec((tm, GW), tab)],
        out_specs=[o[0] for o in outs],
        out_shape=[o[1] for o in outs],
        compiler_params=_params("parallel"),
        name="inproj",
    )(x2, g, w_all, wvt, ca, sa, cd, sd)


M_INIT = -1e30


def _att_init(q, lane, combos, qm_ref, m_ref, l_ref, acc_ref):
    for c, (width, idx, _) in enumerate(combos):
        qm_ref[c] = jnp.where((lane // width) == idx, q, jnp.zeros_like(q))
    m_ref[...] = jnp.full_like(m_ref, M_INIT)
    l_ref[...] = jnp.zeros_like(l_ref)
    acc_ref[...] = jnp.zeros_like(acc_ref)


def _att_update(c, st, vt_sub, m_ref, l_ref, acc_ref):
    ms = slice(c * 8, c * 8 + 1)
    rows = slice(c * HEAD_DIM, (c + 1) * HEAD_DIM)
    m_old = m_ref[ms, :]
    m_new = jnp.maximum(m_old, jnp.max(st, axis=0, keepdims=True))
    alpha = jnp.exp2(m_old - m_new)
    pt = jnp.exp2(st - m_new)
    l_ref[ms, :] = alpha * l_ref[ms, :] + jnp.sum(pt, axis=0, keepdims=True)
    acc_ref[rows, :] = alpha * acc_ref[rows, :] + _dot(vt_sub, pt.astype(BF16))
    m_ref[ms, :] = m_new


def _att_keyblock(k_ref, vt_ref, n, combos, qm_ref, m_ref, l_ref, acc_ref, mask_fn):
    steps = [(piece * ATT_SUB, c) for piece in range(ATT_BLOCK // ATT_SUB) for c in range(len(combos))]

    def scores(key0, c):
        kb = k_ref[0, pl.ds(pl.multiple_of(n * ATT_BLOCK + key0, ATT_SUB), ATT_SUB), :]
        return _dot_nt(kb, qm_ref[c])

    st_next = scores(*steps[0])
    for i, (key0, c) in enumerate(steps):
        st = st_next
        if i + 1 < len(steps):
            st_next = scores(*steps[i + 1])
        head = combos[c][2]
        vt_sub = vt_ref[0, n, head * HEAD_DIM:(head + 1) * HEAD_DIM, key0:key0 + ATT_SUB]
        _att_update(c, mask_fn(c, key0, st), vt_sub, m_ref, l_ref, acc_ref)


def _causal_mask(c, key0, st):
    krow = key0 + lax.broadcasted_iota(jnp.int32, st.shape, 0)
    qcol = lax.broadcasted_iota(jnp.int32, st.shape, 1)
    return jnp.where(krow <= qcol, st, NEG_INF)


def _att_result(c, l_ref, acc_ref):
    return acc_ref[c * HEAD_DIM:(c + 1) * HEAD_DIM, :] / l_ref[c * 8:c * 8 + 1, :]


def _moba_kernel(q_ref, k_ref, vt_ref, avg_ref, o_ref, qm_ref, m_ref, l_ref, acc_ref, sel_ref, ot_ref, km_ref,
                 *, nb, n_sel):
    blk = ATT_BLOCK
    j = pl.program_id(1)
    combos = [(HEAD_DIM, h, h) for h in range(N_HEADS)]
    lane = lax.broadcasted_iota(jnp.int32, (1, GW), 1)
    _att_init(q_ref[0], lane, combos, qm_ref, m_ref, l_ref, acc_ref)

    @pl.when(j == 0)
    def _():
        kmean = _dot(avg_ref[...], k_ref[0])
        for c in range(N_HEADS):
            kmc = jnp.where((lane // HEAD_DIM) == c, kmean, 0.0)
            hi = kmc.astype(BF16)
            km_ref[c * 16:c * 16 + 8, :] = hi.astype(F32)
            km_ref[c * 16 + 8:(c + 1) * 16, :] = kmc - hi.astype(F32)

    gates = _dot_nt(km_ref[...].astype(BF16), q_ref[0])
    nrow = lax.broadcasted_iota(jnp.int32, (8, blk), 0)
    past = nrow < j
    for c in range(N_HEADS):
        gate = jnp.where(past, gates[c * 16:c * 16 + 8] + gates[c * 16 + 8:(c + 1) * 16], NEG_INF)
        rank = jnp.zeros((8, blk), F32)
        for mblk in range(nb):
            gm = gate[mblk:mblk + 1, :]
            rank = rank + jnp.where(gm > gate, 1.0, jnp.where((gm == gate) & (nrow > mblk), 1.0, 0.0))
        sel_ref[c * 8:(c + 1) * 8, :] = jnp.where(past & (rank < n_sel), 1.0, 0.0)

    def body(n, carry):
        def selected(c, key0, st):
            return jnp.where(sel_ref[pl.ds(c * 8 + n, 1), :] > 0.5, st, NEG_INF)
        _att_keyblock(k_ref, vt_ref, n, combos, qm_ref, m_ref, l_ref, acc_ref, selected)
        return carry

    lax.fori_loop(0, j, body, 0)
    _att_keyblock(k_ref, vt_ref, j, combos, qm_ref, m_ref, l_ref, acc_ref, _causal_mask)
    for c in range(N_HEADS):
        ot_ref[c * HEAD_DIM:(c + 1) * HEAD_DIM, :] = _att_result(c, l_ref, acc_ref)
    o_ref[0] = ot_ref[...].T.astype(BF16)


def _att_specs(b, s, n_combos):
    blk = ATT_BLOCK
    nb = s // blk
    in_specs = [pl.BlockSpec((1, blk, GW), lambda i, j: (i, j, 0)),
                pl.BlockSpec((1, s, GW), lambda i, j: (i, 0, 0)),
                pl.BlockSpec((1, nb, GW, blk), lambda i, j: (i, 0, 0, 0))]
    out_spec = pl.BlockSpec((1, blk, GW), lambda i, j: (i, j, 0))
    scratch = [pltpu.VMEM((n_combos, blk, GW), BF16),
               pltpu.VMEM((n_combos * 8, blk), F32), pltpu.VMEM((n_combos * 8, blk), F32),
               pltpu.VMEM((n_combos * HEAD_DIM, blk), F32)]
    return in_specs, out_spec, scratch


def _moba(q, k, vt):
    b, s, _ = q.shape
    blk = ATT_BLOCK
    nb = s // blk
    assert nb <= 8
    n_sel = min(MOBA_TOPK, max(nb - 1, 1))
    avg = jnp.where(jnp.arange(s)[None, :] // blk == jnp.arange(8)[:, None], 1.0 / blk, 0.0).astype(BF16)
    in_specs, out_spec, scratch = _att_specs(b, s, N_HEADS)
    return pl.pallas_call(
        functools.partial(_moba_kernel, nb=nb, n_sel=n_sel),
        grid=(b, nb),
        in_specs=in_specs + [pl.BlockSpec((8, s), lambda i, j: (0, 0))],
        out_specs=out_spec,
        out_shape=jax.ShapeDtypeStruct((b, s, GW), BF16),
        scratch_shapes=scratch + [pltpu.VMEM((N_HEADS * 8, blk), F32), pltpu.VMEM((GW, blk), F32),
                                  pltpu.VMEM((N_HEADS * 16, GW), F32)],
        compiler_params=_params("parallel", "arbitrary"),
        name="moba",
    )(q, k, vt.reshape(b, nb, GW, blk), avg)


def _diff_kernel(q_ref, k_ref, vt_ref, lqk_ref, g_ref, o_ref, qm_ref, m_ref, l_ref, acc_ref, ot_ref,
                 *, lambda_init):
    blk = ATT_BLOCK
    j = pl.program_id(1)
    combos = [(DIFF_QK_DIM, 2 * h + i, h) for h in range(N_HEADS) for i in range(2)]
    lane = lax.broadcasted_iota(jnp.int32, (1, GW), 1)
    _att_init(q_ref[0], lane, combos, qm_ref, m_ref, l_ref, acc_ref)

    def body(n, carry):
        _att_keyblock(k_ref, vt_ref, n, combos, qm_ref, m_ref, l_ref, acc_ref, lambda c, key0, st: st)
        return carry

    lax.fori_loop(0, j, body, 0)
    _att_keyblock(k_ref, vt_ref, j, combos, qm_ref, m_ref, l_ref, acc_ref, _causal_mask)

    lqk = lqk_ref[...]
    lam = (jnp.exp(jnp.sum(lqk[0:1] * lqk[1:2], axis=1, keepdims=True))
           - jnp.exp(jnp.sum(lqk[2:3] * lqk[3:4], axis=1, keepdims=True)) + lambda_init)
    for h in range(N_HEADS):
        x = _att_result(2 * h, l_ref, acc_ref) - lam * _att_result(2 * h + 1, l_ref, acc_ref)
        ms = jnp.mean(x * x, axis=0, keepdims=True)
        ot_ref[h * HEAD_DIM:(h + 1) * HEAD_DIM, :] = x * lax.rsqrt(ms + DIFF_SUBLN_EPS)
    o_ref[0] = (ot_ref[...].T * g_ref[...] * (1.0 - lambda_init)).astype(BF16)


def _diff(q, k, vt, lqk, g_tiled, lambda_init):
    b, s, _ = q.shape
    blk = ATT_BLOCK
    nb = s // blk
    in_specs, out_spec, scratch = _att_specs(b, s, 2 * N_HEADS)
    return pl.pallas_call(
        functools.partial(_diff_kernel, lambda_init=lambda_init),
        grid=(b, nb),
        in_specs=in_specs + [pl.BlockSpec((4, 128), lambda i, j: (0, 0)),
                             pl.BlockSpec((1, GW), lambda i, j: (0, 0))],
        out_specs=out_spec,
        out_shape=jax.ShapeDtypeStruct((b, s, GW), BF16),
        scratch_shapes=scratch + [pltpu.VMEM((GW, blk), F32)],
        compiler_params=_params("parallel", "arbitrary"),
        name="diffattn",
    )(q, k, vt.reshape(b, nb, GW, blk), lqk, g_tiled)


def _sgu_kernel(pc_ref, lng_ref, lnb_ref, w_ref, bias_ref, o_ref):
    rows = pc_ref.shape[0]
    pc = pc_ref[...].astype(F32)
    u = _gelu(pc[:, :GW])
    v = _gelu(pc[:, GW:])
    mu = jnp.mean(v, axis=-1, keepdims=True)
    d = v - mu
    var = jnp.mean(d * d, axis=-1, keepdims=True)
    vn = d * lax.rsqrt(var + 1e-5) * lng_ref[...] + lnb_ref[...]
    lane = lax.broadcasted_iota(jnp.int32, (1, GW), 1)
    tr = lax.broadcasted_iota(jnp.int32, (SGU_CHUNK, SGU_CHUNK), 0)
    tc = lax.broadcasted_iota(jnp.int32, (SGU_CHUNK, SGU_CHUNK), 1)
    wm = [jnp.where(tr >= tc, w_ref[h], 0.0).astype(BF16) for h in range(N_HEADS)]
    for c in range(rows // SGU_CHUNK):
        sl = slice(c * SGU_CHUNK, (c + 1) * SGU_CHUNK)
        vc = vn[sl]
        mixed = bias_ref[...]
        for h in range(N_HEADS):
            vm = jnp.where((lane // HEAD_DIM) == h, vc, 0.0).astype(BF16)
            mixed = mixed + _dot(wm[h], vm)
        o_ref[sl, :] = (u[sl] * mixed).astype(BF16)


def _sgu(pc, lng, lnb, w, bias2d):
    t = pc.shape[0]
    tm = ROW_TILE
    const2 = lambda i: (0, 0)
    return pl.pallas_call(
        _sgu_kernel,
        grid=(t // tm,),
        in_specs=[pl.BlockSpec((tm, 2 * GW), lambda i: (i, 0)),
                  pl.BlockSpec((1, GW), const2), pl.BlockSpec((1, GW), const2),
                  pl.BlockSpec((N_HEADS, SGU_CHUNK, SGU_CHUNK), lambda i: (0, 0, 0)),
                  pl.BlockSpec((SGU_CHUNK, GW), const2)],
        out_specs=pl.BlockSpec((tm, GW), lambda i: (i, 0)),
        out_shape=jax.ShapeDtypeStruct((t, GW), BF16),
        compiler_params=_params("parallel"),
        name="sgu",
    )(pc, lng, lnb, w, bias2d)


def _rwkv_kernel(pb_ref, mu_ref, wl_ref, w0_ref, a0_ref, kk_ref, ka_ref, rk_ref, lng_ref, lnb_ref,
                 bd_ref, ltri_ref, o_ref,
                 state_ref, prev_ref, r_s, lw_s, k_s, v_s, al_s, be_s, y_s):
    rt = pb_ref.shape[1]
    c_len = RWKV_CHUNK

    @pl.when(pl.program_id(1) == 0)
    def _():
        state_ref[...] = jnp.zeros_like(state_ref)
        prev_ref[...] = jnp.zeros_like(prev_ref)

    bd = bd_ref[...]
    xt = pb_ref[0].astype(F32)
    rowid = lax.broadcasted_iota(jnp.int32, (rt, 1), 0)
    prev = jnp.where(rowid == 0, prev_ref[...], pltpu.roll(xt, 1, axis=0))
    prev_ref[...] = xt[rt - 1:rt, :]
    xs = xt + (prev - xt) * mu_ref[...]
    r = xs[:, 0:GW]
    k = xs[:, GW:2 * GW]
    v = xs[:, 2 * GW:3 * GW]
    lo = xs[:, 3 * GW:4 * GW]
    lane = lax.broadcasted_iota(jnp.int32, (1, GW), 1)
    lin = jnp.where(lane < 64, jnp.tanh(lo), jnp.where(lane < 128, lo, _sigmoid(lo)))
    z = _dot(lin.astype(BF16), wl_ref[...])
    wz = -(w0_ref[...] + z[:, 0:GW])
    softplus = jnp.maximum(wz, 0.0) + jnp.log(1.0 + jnp.exp(-jnp.abs(wz)))
    lw = -jnp.exp(-softplus - 0.5)
    a = _sigmoid(a0_ref[...] + z[:, GW:2 * GW])
    gate = z[:, 2 * GW:3 * GW]
    kk = k * kk_ref[...]
    kk = kk / jnp.maximum(jnp.sqrt(_segsum(kk * kk, bd)), 1e-12)
    k2 = k * (1.0 + (a - 1.0) * ka_ref[...])
    bonus = _segsum(r * k2 * rk_ref[...], bd) * v

    r_s[...] = r
    lw_s[...] = lw
    k_s[...] = k2
    v_s[...] = v
    al_s[...] = -kk
    be_s[...] = kk * a

    hms = [(lane // HEAD_DIM) == h for h in range(N_HEADS)]
    brow = lax.broadcasted_iota(jnp.int32, (GW, GW), 0)
    bcol = lax.broadcasted_iota(jnp.int32, (GW, GW), 1)
    strict = brow > bcol
    lower = brow >= bcol
    eye = brow == bcol
    ltri = ltri_ref[...]

    def big(x):
        return jnp.concatenate([jnp.where(hm, x, 0.0) for hm in hms], axis=0)

    def unbig(x):
        return x[0:c_len] + x[c_len:2 * c_len] + x[2 * c_len:3 * c_len] + x[3 * c_len:4 * c_len]

    def chunk(c, carry):
        sl = pl.ds(pl.multiple_of(c * c_len, c_len), c_len)
        lwc = lw_s[sl, :]
        h3 = _split3(lwc)
        cs = _dot(ltri, h3[0]) + _dot(ltri, h3[1]) + _dot(ltri, h3[2])
        cs_end = cs[c_len - 1:c_len, :]
        e_neg = jnp.exp(-cs)
        e_rem = jnp.exp(cs_end - cs)
        al, be, kc, rc, vc = al_s[sl, :], be_s[sl, :], k_s[sl, :], r_s[sl, :], v_s[sl, :]
        r_t = rc * jnp.exp(cs)
        a_b = big(al * jnp.exp(cs - lwc)).astype(BF16)
        r_b = big(r_t).astype(BF16)
        bi_b = big(be * e_neg).astype(BF16)
        ki_b = big(kc * e_neg).astype(BF16)
        v_b = big(vc).astype(BF16)
        a_ab = jnp.where(strict, _dot_nt(a_b, bi_b), 0.0)
        a_ak = jnp.where(strict, _dot_nt(a_b, ki_b), 0.0)
        a_rb = jnp.where(lower, _dot_nt(r_b, bi_b), 0.0).astype(BF16)
        a_rk = jnp.where(lower, _dot_nt(r_b, ki_b), 0.0).astype(BF16)
        p = a_ab
        tinv = jnp.where(eye, 1.0, a_ab)
        for _ in range(5):
            pb = p.astype(BF16)
            p = _dot(pb, pb)
            tinv = tinv + _dot(tinv.astype(BF16), p.astype(BF16))
        tb = tinv.astype(BF16)
        av = _dot(a_ak.astype(BF16), v_b)
        p1 = _dot(tb, a_b).astype(BF16)
        p2 = _dot(tb, av.astype(BF16)).astype(BF16)
        q_tok = r_t + unbig(_dot(a_rb, p1))
        z_tok = unbig(_dot(a_rb, p2) + _dot(a_rk, v_b))
        bd_t = big(be * e_rem).T.astype(BF16)
        kd_t = big(kc * e_rem).T.astype(BF16)
        m_c = jnp.where(eye, jnp.exp(cs_end), 0.0) + _dot(bd_t, p1)
        n_c = _dot(bd_t, p2) + _dot(kd_t, v_b)
        skb = state_ref[...].astype(BF16)
        y_s[sl, :] = _dot(q_tok.astype(BF16), skb) + z_tok
        state_ref[...] = _dot(m_c.astype(BF16), skb) + n_c
        return carry

    lax.fori_loop(0, rt // c_len, chunk, 0)

    y = y_s[...]
    mean = _segsum(y, bd) * (1.0 / HEAD_DIM)
    d = y - mean
    var = _segsum(d * d, bd) * (1.0 / HEAD_DIM)
    yn = d * lax.rsqrt(var + RWKV_LN_EPS) * lng_ref[...] + lnb_ref[...]
    o_ref[0] = ((yn + bonus) * gate).astype(BF16)


def _rwkv(pb, mu, wl, w0, a0, kk, ka, rk, lng, lnb, bd, ltri):
    b, s, _ = pb.shape
    rt = RWKV_ROWS
    c2 = lambda i, j: (0, 0)
    vec = pl.BlockSpec((1, GW), c2)
    return pl.pallas_call(
        _rwkv_kernel,
        grid=(b, s // rt),
        in_specs=[pl.BlockSpec((1, rt, 4 * GW), lambda i, j: (i, j, 0)),
                  pl.BlockSpec((1, 4 * GW), c2),
                  pl.BlockSpec((GW, 3 * GW), c2),
                  vec, vec, vec, vec, vec, vec, vec,
                  pl.BlockSpec((GW, GW), c2),
                  pl.BlockSpec((RWKV_CHUNK, RWKV_CHUNK), c2)],
        out_specs=pl.BlockSpec((1, rt, GW), lambda i, j: (i, j, 0)),
        out_shape=jax.ShapeDtypeStruct((b, s, GW), BF16),
        scratch_shapes=[pltpu.VMEM((GW, GW), F32), pltpu.VMEM((1, 4 * GW), F32)]
                       + [pltpu.VMEM((rt, GW), F32) for _ in range(7)],
        compiler_params=_params("parallel", "arbitrary"),
        name="rwkv7",
    )(pb, mu, wl, w0, a0, kk, ka, rk, lng, lnb, bd, ltri)


def _outproj_kernel(oa_ref, ob_ref, oc_ref, od_ref, x_ref, w_ref, g_ref, wrh_ref, wrl_ref, br_ref,
                    h_o, hn_o, eid_o, wts_o):
    mixed = jnp.concatenate([oa_ref[...], ob_ref[...], oc_ref[...], od_ref[...]], axis=1)
    h = x_ref[...] + _dot(mixed, w_ref[...])
    h_o[...] = h
    ms = jnp.mean(h * h, axis=-1, keepdims=True)
    hn = h * lax.rsqrt(ms + RMS_EPS) * g_ref[...]
    hn_o[...] = hn
    hi = hn.astype(BF16)
    lo = (hn - hi.astype(F32)).astype(BF16)
    wrh = wrh_ref[...]
    logits = _dot(hi, wrh) + _dot(lo, wrh) + _dot(hi, wrl_ref[...]) + br_ref[...]
    tm = logits.shape[0]
    lane = lax.broadcasted_iota(jnp.int32, (tm, 128), 1)
    lanef = lane.astype(F32)
    big_idx = 1e9

    def rmax(t):
        return jnp.max(t, axis=1, keepdims=True)

    def rmin(t):
        return jnp.min(t, axis=1, keepdims=True)

    def rsum(t):
        return jnp.sum(t, axis=1, keepdims=True)

    gmask = lane < N_EXPERT_GROUPS
    gl = jnp.where(gmask, logits, NEG_INF)
    gmax = rmax(gl)
    gidx = rmin(jnp.where(gl == gmax, lanef, big_idx))
    gp = 1.0 / rsum(jnp.where(gmask, jnp.exp(logits - gmax), 0.0))
    e_lo = N_EXPERT_GROUPS + gidx * EXPERTS_PER_GROUP
    emask = (lanef >= e_lo) & (lanef < e_lo + EXPERTS_PER_GROUP)
    el = jnp.where(emask, logits, NEG_INF)
    emax = rmax(el)
    esum = rsum(jnp.where(emask, jnp.exp(logits - emax), 0.0))
    i1 = rmin(jnp.where(el == emax, lanef, big_idx))
    el2 = jnp.where(lanef == i1, NEG_INF, el)
    m2 = rmax(el2)
    i2 = rmin(jnp.where(el2 == m2, lanef, big_idx))
    p1 = 1.0 / esum
    p2 = jnp.exp(m2 - emax) / esum
    w1 = gp * (p1 / (p1 + p2))
    w2 = gp * (p2 / (p1 + p2))
    eid = jnp.where(lane == 0, i1 - N_EXPERT_GROUPS, jnp.where(lane == 1, i2 - N_EXPERT_GROUPS, 0.0))
    eid_o[...] = eid.astype(jnp.int32)
    wts_o[...] = jnp.where(lane == 0, w1, jnp.where(lane == 1, w2, 0.0))


def _outproj(oa, ob, oc, od, x2, w_out, g2, wr_hi, wr_lo, br):
    t = x2.shape[0]
    tm = ROW_TILE
    row = lambda i: (i, 0)
    const = lambda i: (0, 0)
    mix = pl.BlockSpec((tm, GW), row)
    return pl.pallas_call(
        _outproj_kernel,
        grid=(t // tm,),
        in_specs=[mix, mix, mix, mix,
                  pl.BlockSpec((tm, D_MODEL), row),
                  pl.BlockSpec((D_MODEL, D_MODEL), const),
                  pl.BlockSpec((1, D_MODEL), const),
                  pl.BlockSpec((D_MODEL, 128), const), pl.BlockSpec((D_MODEL, 128), const),
                  pl.BlockSpec((1, 128), const)],
        out_specs=[pl.BlockSpec((tm, D_MODEL), row), pl.BlockSpec((tm, D_MODEL), row),
                   pl.BlockSpec((tm, 128), row), pl.BlockSpec((tm, 128), row)],
        out_shape=[jax.ShapeDtypeStruct((t, D_MODEL), F32), jax.ShapeDtypeStruct((t, D_MODEL), F32),
                   jax.ShapeDtypeStruct((t, 128), jnp.int32), jax.ShapeDtypeStruct((t, 128), F32)],
        compiler_params=_params("parallel"),
        name="outproj_router",
    )(oa, ob, oc, od, x2, w_out, g2, wr_hi, wr_lo, br)


def _expert_kernel(blk_e_ref, blk_nv_ref, tok_ref, tokn_ref, dst_ref, hn_hbm, wg_ref, wu_ref, wd_ref, y_hbm,
                   xbuf, ybuf, gsem, ssem):
    i = pl.program_id(0)
    nblk = pl.num_programs(0)
    slot = i % 2
    other = 1 - slot
    full = xbuf.shape[1]
    nv = blk_nv_ref[i]
    nv_next = jnp.where(i + 1 < nblk, blk_nv_ref[jnp.minimum(i + 1, nblk - 1)], 0)
    nv_prev = jnp.where(i >= 1, blk_nv_ref[jnp.maximum(i - 1, 0)], 0)
    nv_prev2 = jnp.where(i >= 2, blk_nv_ref[jnp.maximum(i - 2, 0)], 0)

    def row_in(idx_ref, s, r):
        return pltpu.make_async_copy(hn_hbm.at[pl.ds(idx_ref[0, 0, r], 1)], xbuf.at[s, pl.ds(r, 1)], gsem.at[s])

    def row_out(s, r):
        return pltpu.make_async_copy(ybuf.at[s, pl.ds(r, 1)], y_hbm.at[pl.ds(dst_ref[0, 0, r], 1)], ssem.at[s])

    def for_rows(n, fn):
        def body(r, carry):
            fn(r)
            return carry

        @pl.when(n == full)
        def _():
            lax.fori_loop(0, full, body, 0, unroll=8)

        @pl.when(n < full)
        def _():
            lax.fori_loop(0, n, body, 0)

    def wait_rows(n, block_copy, row_copy):
        @pl.when(n == full)
        def _():
            block_copy.wait()

        @pl.when(n < full)
        def _():
            lax.fori_loop(0, n, lambda r, carry: (row_copy.wait(), carry)[1], 0)

    def wait_gather(n, s):
        wait_rows(n, pltpu.make_async_copy(hn_hbm.at[pl.ds(0, full)], xbuf.at[s], gsem.at[s]),
                  pltpu.make_async_copy(hn_hbm.at[pl.ds(0, 1)], xbuf.at[s, pl.ds(0, 1)], gsem.at[s]))

    def wait_scatter(n, s):
        wait_rows(n, pltpu.make_async_copy(ybuf.at[s], y_hbm.at[pl.ds(0, full)], ssem.at[s]),
                  pltpu.make_async_copy(ybuf.at[s, pl.ds(0, 1)], y_hbm.at[pl.ds(0, 1)], ssem.at[s]))

    @pl.when(i == 0)
    def _():
        xbuf[...] = jnp.zeros_like(xbuf)
        for_rows(nv, lambda r: row_in(tok_ref, slot, r).start())

    for_rows(nv_next, lambda r: row_in(tokn_ref, other, r).start())
    wait_gather(nv, slot)
    wait_scatter(nv_prev2, slot)

    @pl.when(nv > 0)
    def _():
        xb = xbuf[slot].astype(BF16)
        hg = _dot(xb, wg_ref[0])
        hu = _dot(xb, wu_ref[0])
        hdn = (hg * _sigmoid(hg) * hu).astype(BF16)
        ybuf[slot] = _dot(hdn, wd_ref[0])
        for_rows(nv, lambda r: row_out(slot, r).start())

    @pl.when(i == nblk - 1)
    def _():
        wait_scatter(nv_prev, other)
        wait_scatter(nv, slot)


def _experts(blk_e, blk_nv, row_tok, row_dst, hn, wg, wu, wd, m_rows):
    nblk = blk_e.shape[0]
    r = MOE_ROWS
    grid_spec = pltpu.PrefetchScalarGridSpec(
        num_scalar_prefetch=2,
        grid=(nblk,),
        in_specs=[pl.BlockSpec((1, 1, r), lambda i, be, nv: (i, 0, 0), memory_space=pltpu.SMEM),
                  pl.BlockSpec((1, 1, r), lambda i, be, nv: (jnp.minimum(i + 1, nblk - 1), 0, 0),
                               memory_space=pltpu.SMEM),
                  pl.BlockSpec((1, 1, r), lambda i, be, nv: (i, 0, 0), memory_space=pltpu.SMEM),
                  pl.BlockSpec(memory_space=pl.ANY),
                  pl.BlockSpec((1, D_MODEL, EXPERT_FF), lambda i, be, nv: (be[i], 0, 0)),
                  pl.BlockSpec((1, D_MODEL, EXPERT_FF), lambda i, be, nv: (be[i], 0, 0)),
                  pl.BlockSpec((1, EXPERT_FF, D_MODEL), lambda i, be, nv: (be[i], 0, 0))],
        out_specs=pl.BlockSpec(memory_space=pl.ANY),
        scratch_shapes=[pltpu.VMEM((2, r, D_MODEL), F32), pltpu.VMEM((2, r, D_MODEL), F32),
                        pltpu.SemaphoreType.DMA((2,)), pltpu.SemaphoreType.DMA((2,))],
    )
    return pl.pallas_call(
        _expert_kernel,
        grid_spec=grid_spec,
        out_shape=jax.ShapeDtypeStruct((m_rows, D_MODEL), F32),
        compiler_params=_params("arbitrary"),
        name="experts",
    )(blk_e, blk_nv, row_tok, row_tok, row_dst, hn, wg, wu, wd)


def _dispatch(eid, t):
    r = MOE_ROWS
    m = 2 * t
    e_flat = eid[:, :2].reshape(m)
    order = jnp.argsort(e_flat, stable=True).astype(jnp.int32)
    counts = jnp.sum((e_flat[:, None] == jnp.arange(N_EXPERTS, dtype=jnp.int32)[None, :]).astype(jnp.int32), axis=0)
    starts = jnp.cumsum(counts) - counts
    pcounts = (counts + r - 1) // r * r
    pends = jnp.cumsum(pcounts)
    pstarts = pends - pcounts
    nblk = m // r + N_EXPERTS
    blk_e = jnp.minimum(jnp.searchsorted(pends, jnp.arange(nblk, dtype=jnp.int32) * r, side="right"),
                        N_EXPERTS - 1).astype(jnp.int32)
    p = jnp.arange(nblk * r, dtype=jnp.int32)
    e_p = blk_e[p // r]
    off = p - pstarts[e_p]
    valid = off < counts[e_p]
    a = order[jnp.clip(starts[e_p] + off, 0, m - 1)]
    row_tok = jnp.where(valid, a // 2, 0).astype(jnp.int32)
    row_dst = jnp.where(valid, (a % 2) * t + a // 2, 0).astype(jnp.int32)
    blk_nv = jnp.sum(valid.reshape(nblk, r).astype(jnp.int32), axis=1)
    return blk_e, blk_nv, row_tok.reshape(nblk, 1, r), row_dst.reshape(nblk, 1, r)


def _combine_kernel(h_ref, y0_ref, y1_ref, wts_ref, fg_ref, o_ref, *, final):
    w = wts_ref[...]
    x = h_ref[...] + w[:, 0:1] * y0_ref[0] + w[:, 1:2] * y1_ref[0]
    if final:
        ms = jnp.mean(x * x, axis=-1, keepdims=True)
        x = x * lax.rsqrt(ms + RMS_EPS) * fg_ref[...]
    o_ref[...] = x


def _combine(h, y, wts, fg, final):
    t = h.shape[0]
    tm = ROW_TILE
    row = lambda i: (i, 0)
    return pl.pallas_call(
        functools.partial(_combine_kernel, final=final),
        grid=(t // tm,),
        in_specs=[pl.BlockSpec((tm, D_MODEL), row),
                  pl.BlockSpec((1, tm, D_MODEL), lambda i: (0, i, 0)),
                  pl.BlockSpec((1, tm, D_MODEL), lambda i: (1, i, 0)),
                  pl.BlockSpec((tm, 128), row),
                  pl.BlockSpec((1, D_MODEL), lambda i: (0, 0))],
        out_specs=pl.BlockSpec((tm, D_MODEL), row),
        out_shape=jax.ShapeDtypeStruct((t, D_MODEL), F32),
        compiler_params=_params("parallel"),
        name="combine",
    )(h, y, y, wts, fg)


def _rope_tables(seq, head_width, rot_dim):
    half = rot_dim // 2
    inv = ROPE_THETA ** (-jnp.arange(0, rot_dim, 2, dtype=F32) / rot_dim)
    ang = jnp.arange(seq, dtype=F32)[:, None] * inv[None, :]
    cos, sin = jnp.cos(ang), jnp.sin(ang)
    d = np.arange(GW) % head_width
    idx = d % half
    cos_t = jnp.where((d < rot_dim)[None, :], cos[:, idx], 1.0)
    sin_t = jnp.where((d < half)[None, :], -sin[:, idx], jnp.where((d < rot_dim)[None, :], sin[:, idx], 0.0))
    return cos_t.astype(F32), sin_t.astype(F32)


def _swap_cols(w, head_width, rot_dim):
    half = rot_dim // 2
    c = np.arange(GW)
    d = c % head_width
    partner = np.where(d < half, c + half, np.where(d < rot_dim, c - half, c))
    keep = (d < rot_dim).astype(np.float32)
    return w[:, partner] * keep[None, :]


def _layer_weights(l, w_in, rwkv_mu, rwkv_w2, rwkv_a2, rwkv_g2, sgu_w, sgu_b, diff_l, diff_subln_g,
                   router_group_w, router_group_b, router_expert_w, router_expert_b):
    wi = w_in[l]
    a0, b0, c0, d0 = 0, 3 * GW, 3 * GW + 4 * GW, 3 * GW + 4 * GW + 2 * GW
    qa, ka, va = wi[:, a0:a0 + GW], wi[:, a0 + GW:a0 + 2 * GW], wi[:, a0 + 2 * GW:a0 + 3 * GW]
    bsl = [(0, 256), (320, 576), (576, 832), (256, 320), (832, 896), (896, 1024)]
    wb = jnp.concatenate([wi[:, b0 + s:b0 + e] for s, e in bsl], axis=1)
    mu = jnp.concatenate([rwkv_mu[l][s:e] for s, e in bsl])[None, :]
    wc = wi[:, c0:c0 + 2 * GW]
    qd, kd, vd = wi[:, d0:d0 + GW], wi[:, d0 + GW:d0 + 2 * GW], wi[:, d0 + 2 * GW:d0 + 3 * GW]
    w_all = jnp.concatenate(
        [qa, _swap_cols(qa, HEAD_DIM, 16), ka, _swap_cols(ka, HEAD_DIM, 16), wb, wc,
         qd, _swap_cols(qd, DIFF_QK_DIM, 8), kd, _swap_cols(kd, DIFF_QK_DIM, 8)], axis=1).astype(BF16)
    wvt = jnp.concatenate([va, vd], axis=1).T.astype(BF16)
    wl = jnp.zeros((GW, 3 * GW), F32)
    wl = wl.at[0:64, 0:GW].set(rwkv_w2[l]).at[64:128, GW:2 * GW].set(rwkv_a2[l]).at[128:256, 2 * GW:].set(rwkv_g2[l])
    bias2d = jnp.repeat(sgu_b[l].T, HEAD_DIM, axis=1)
    lqk = jnp.zeros((4, 128), F32).at[:, :DIFF_QK_DIM].set(jnp.stack([d[l] for d in diff_l]))
    wr = jnp.zeros((D_MODEL, 128), F32).at[:, :N_EXPERT_GROUPS].set(router_group_w[l])
    wr = wr.at[:, N_EXPERT_GROUPS:N_EXPERT_GROUPS + N_EXPERTS].set(router_expert_w[l])
    wr_hi = wr.astype(BF16)
    wr_lo = (wr - wr_hi.astype(F32)).astype(BF16)
    br = jnp.zeros((1, 128), F32).at[0, :N_EXPERT_GROUPS].set(router_group_b[l])
    br = br.at[0, N_EXPERT_GROUPS:N_EXPERT_GROUPS + N_EXPERTS].set(router_expert_b[l])
    return dict(w_all=w_all, wvt=wvt, mu=mu, wl=wl.astype(BF16), bias2d=bias2d, lqk=lqk,
                subln=jnp.tile(diff_subln_g[l], N_HEADS)[None, :], wr_hi=wr_hi, wr_lo=wr_lo, br=br)


def kernel(x, norm1_g, w_in, rwkv_mu, rwkv_w0, rwkv_w2, rwkv_a0, rwkv_a2, rwkv_g2, rwkv_kk, rwkv_ka, rwkv_rk, rwkv_ln_g, rwkv_ln_b, sgu_ln_g, sgu_ln_b, sgu_w, sgu_b, diff_lq1, diff_lk1, diff_lq2, diff_lk2, diff_subln_g, w_out, norm2_g, router_group_w, router_group_b, router_expert_w, router_expert_b, moe_w_gate, moe_w_up, moe_w_down, final_g):
    b, s, d = x.shape
    depth = w_in.shape[0]
    t = b * s
    ca, sa = _rope_tables(s, HEAD_DIM, 16)
    cd, sd = _rope_tables(s, DIFF_QK_DIM, 8)
    seg = np.arange(GW) // HEAD_DIM
    bd = jnp.asarray(seg[:, None] == seg[None, :], BF16)
    ltri = jnp.asarray(np.tril(np.ones((RWKV_CHUNK, RWKV_CHUNK))), BF16)
    row = lambda p: p[None, :]
    x2 = x.reshape(t, d)
    for l in range(depth):
        lw = _layer_weights(l, w_in, rwkv_mu, rwkv_w2, rwkv_a2, rwkv_g2, sgu_w, sgu_b,
                            (diff_lq1, diff_lk1, diff_lq2, diff_lk2), diff_subln_g,
                            router_group_w, router_group_b, router_expert_w, router_expert_b)
        qa, ka, va, pb, pc, qd, kd, vd = _inproj(x2, row(norm1_g[l]), lw["w_all"], lw["wvt"], ca, sa, cd, sd, s)
        seq3 = lambda z: z.reshape(b, s, z.shape[-1])
        out_a = _moba(seq3(qa), seq3(ka), va)
        out_b = _rwkv(seq3(pb), lw["mu"], lw["wl"], row(rwkv_w0[l]), row(rwkv_a0[l]), row(rwkv_kk[l]),
                      row(rwkv_ka[l]), row(rwkv_rk[l].reshape(GW)), row(rwkv_ln_g[l]), row(rwkv_ln_b[l]), bd, ltri)
        out_c = _sgu(pc, row(sgu_ln_g[l]), row(sgu_ln_b[l]), sgu_w[l], lw["bias2d"])
        lambda_init = 0.8 - 0.6 * math.exp(-0.3 * l)
        out_d = _diff(seq3(qd), seq3(kd), vd, lw["lqk"], lw["subln"], lambda_init)
        h, hn, eid, wts = _outproj(out_a.reshape(t, GW), out_b.reshape(t, GW), out_c, out_d.reshape(t, GW),
                                   x2, w_out[l].astype(BF16), row(norm2_g[l]), lw["wr_hi"], lw["wr_lo"], lw["br"])
        blk_e, blk_nv, row_tok, row_dst = _dispatch(eid, t)
        y = _experts(blk_e, blk_nv, row_tok, row_dst, hn, moe_w_gate[l].astype(BF16),
                     moe_w_up[l].astype(BF16), moe_w_down[l].astype(BF16), 2 * t)
        x2 = _combine(h, y.reshape(2, t, d), wts, row(final_g), l == depth - 1)
    return x2.reshape(b, s, d)
```

```python
import functools
import math

import numpy as np
import jax
import jax.numpy as jnp
from jax import lax
from jax.experimental import pallas as pl
from jax.experimental.pallas import tpu as pltpu

F32 = jnp.float32
BF16 = jnp.bfloat16

D_MODEL = 1024
HEAD_DIM = 64
N_HEADS = 4
GW = N_HEADS * HEAD_DIM
ROPE_THETA = 500000.0
RMS_EPS = 1e-6

ATT_BLOCK = 256
ATT_SUB = 256
LOG2E = 1.4426950408889634
MOBA_TOPK = 3
RWKV_LN_EPS = 64e-5
RWKV_CHUNK = 64
RWKV_ROWS = 256
SGU_CHUNK = 128
DIFF_QK_DIM = 32
DIFF_SUBLN_EPS = 1e-5

N_EXPERT_GROUPS = 4
EXPERTS_PER_GROUP = 8
N_EXPERTS = 32
EXPERT_FF = 512
MOE_ROWS = 256

ROW_TILE = 512
VMEM_LIMIT = 56 * 1024 * 1024

NEG_INF = float("-inf")


def _dot(a, b):
    return jnp.dot(a, b, preferred_element_type=F32)


def _dot_nt(a, b):
    return lax.dot_general(a, b, (((1,), (1,)), ((), ())), preferred_element_type=F32)


def _split3(x):
    hi = x.astype(BF16)
    r1 = x - hi.astype(F32)
    mid = r1.astype(BF16)
    lo = (r1 - mid.astype(F32)).astype(BF16)
    return hi, mid, lo


def _segsum(x, bd):
    hi, mid, lo = _split3(x)
    return _dot(hi, bd) + _dot(mid, bd) + _dot(lo, bd)


def _sigmoid(x):
    return 1.0 / (1.0 + jnp.exp(-x))


def _gelu(x):
    return 0.5 * x * (1.0 + lax.erf(x * (1.0 / math.sqrt(2.0))))


ROW_TILE_ROWS = D_MODEL // 128


def _store_row_tiles(ref, x):
    rows = x.shape[0]
    for k in range(ROW_TILE_ROWS):
        ref[pl.ds(k, rows, stride=ROW_TILE_ROWS), :] = x[:, k * 128:(k + 1) * 128]


def _load_row_tiles(ref, rows):
    return jnp.concatenate([ref[pl.ds(k, rows, stride=ROW_TILE_ROWS), :] for k in range(ROW_TILE_ROWS)], axis=1)


def _params(*sem):
    return pltpu.CompilerParams(dimension_semantics=sem, vmem_limit_bytes=VMEM_LIMIT)


def _inproj_kernel(x_ref, g_ref, w_ref, wvt_ref, ca_ref, sa_ref, cd_ref, sd_ref,
                   qa_o, ka_o, vta_o, pb_o, pc_o, qd_o, kd_o, vtd_o):
    x = x_ref[...]
    ms = jnp.mean(x * x, axis=-1, keepdims=True)
    xn = (x * lax.rsqrt(ms + RMS_EPS) * g_ref[...]).astype(BF16)

    def proj(c0, n=1):
        return _dot(xn, w_ref[:, c0 * GW:(c0 + n) * GW])

    ca, sa = ca_ref[...], sa_ref[...]
    qa_o[...] = ((proj(0) * ca + proj(1) * sa) * (HEAD_DIM ** -0.5 * LOG2E)).astype(BF16)
    ka_o[...] = (proj(2) * ca + proj(3) * sa).astype(BF16)
    pb_o[...] = proj(4, 4).astype(BF16)
    pc_o[...] = proj(8, 2).astype(BF16)
    cd, sd = cd_ref[...], sd_ref[...]
    qd_o[...] = ((proj(10) * cd + proj(11) * sd) * (DIFF_QK_DIM ** -0.5 * LOG2E)).astype(BF16)
    kd_o[...] = (proj(12) * cd + proj(13) * sd).astype(BF16)
    vt = _dot_nt(wvt_ref[...], xn)
    for kb in range(x.shape[0] // ATT_BLOCK):
        cols = slice(kb * ATT_BLOCK, (kb + 1) * ATT_BLOCK)
        vta_o[kb] = vt[0:GW, cols].astype(BF16)
        vtd_o[kb] = vt[GW:2 * GW, cols].astype(BF16)


def _inproj(x2, g, w_all, wvt, ca, sa, cd, sd, seq):
    t = x2.shape[0]
    tm = min(ROW_TILE, seq)
    per_seq = seq // tm
    kbs = tm // ATT_BLOCK
    row = lambda i: (i, 0)
    tab = lambda i: (i % per_seq, 0)
    const = lambda i: (0, 0)
    vt_spec = pl.BlockSpec((kbs, GW, ATT_BLOCK), lambda i: (i, 0, 0))
    vt_shape = jax.ShapeDtypeStruct((t // ATT_BLOCK, GW, ATT_BLOCK), BF16)
    rows = lambda w: (pl.BlockSpec((tm, w), row), jax.ShapeDtypeStruct((t, w), BF16))
    outs = [rows(GW), rows(GW), (vt_spec, vt_shape), rows(4 * GW), rows(2 * GW), rows(GW), rows(GW),
            (vt_spec, vt_shape)]
    return pl.pallas_call(
        _inproj_kernel,
        grid=(t // tm,),
        in_specs=[pl.BlockSpec((tm, D_MODEL), row),
                  pl.BlockSpec((1, D_MODEL), const),
                  pl.BlockSpec((D_MODEL, 14 * GW), const),
                  pl.BlockSpec((2 * GW, D_MODEL), const),
                  pl.BlockSpec((tm, GW), tab), pl.BlockSpec((tm, GW), tab),
                  pl.BlockSpec((tm, GW), tab), pl.BlockSpec((tm, GW), tab)],
        out_specs=[o[0] for o in outs],
        out_shape=[o[1] for o in outs],
        compiler_params=_params("parallel"),
        name="inproj",
    )(x2, g, w_all, wvt, ca, sa, cd, sd)


M_INIT = -1e30


def _att_init(q, lane, combos, qm_ref, m_ref, l_ref, acc_ref):
    for c, (width, idx, _) in enumerate(combos):
        qm_ref[c] = jnp.where((lane // width) == idx, q, jnp.zeros_like(q))
    m_ref[...] = jnp.full_like(m_ref, M_INIT)
    l_ref[...] = jnp.zeros_like(l_ref)
    acc_ref[...] = jnp.zeros_like(acc_ref)


def _att_update(c, st, vt_sub, m_ref, l_ref, acc_ref):
    ms = slice(c * 8, c * 8 + 1)
    rows = slice(c * HEAD_DIM, (c + 1) * HEAD_DIM)
    m_old = m_ref[ms, :]
    m_new = jnp.maximum(m_old, jnp.max(st, axis=0, keepdims=True))
    alpha = jnp.exp2(m_old - m_new)
    pt = jnp.exp2(st - m_new)
    l_ref[ms, :] = alpha * l_ref[ms, :] + jnp.sum(pt, axis=0, keepdims=True)
    acc_ref[rows, :] = alpha * acc_ref[rows, :] + _dot(vt_sub, pt.astype(BF16))
    m_ref[ms, :] = m_new


def _att_keyblock(k_ref, vt_ref, n, combos, qm_ref, m_ref, l_ref, acc_ref, mask_fn):
    steps = [(piece * ATT_SUB, c) for piece in range(ATT_BLOCK // ATT_SUB) for c in range(len(combos))]

    def scores(key0, c):
        kb = k_ref[0, pl.ds(pl.multiple_of(n * ATT_BLOCK + key0, ATT_SUB), ATT_SUB), :]
        return _dot_nt(kb, qm_ref[c])

    st_next = scores(*steps[0])
    for i, (key0, c) in enumerate(steps):
        st = st_next
        if i + 1 < len(steps):
            st_next = scores(*steps[i + 1])
        head = combos[c][2]
        vt_sub = vt_ref[0, n, head * HEAD_DIM:(head + 1) * HEAD_DIM, key0:key0 + ATT_SUB]
        _att_update(c, mask_fn(c, key0, st), vt_sub, m_ref, l_ref, acc_ref)


def _causal_mask(c, key0, st):
    krow = key0 + lax.broadcasted_iota(jnp.int32, st.shape, 0)
    qcol = lax.broadcasted_iota(jnp.int32, st.shape, 1)
    return jnp.where(krow <= qcol, st, NEG_INF)


def _att_result(c, l_ref, acc_ref):
    return acc_ref[c * HEAD_DIM:(c + 1) * HEAD_DIM, :] / l_ref[c * 8:c * 8 + 1, :]


def _moba_kernel(q_ref, k_ref, vt_ref, avg_ref, o_ref, qm_ref, m_ref, l_ref, acc_ref, sel_ref, ot_ref, km_ref,
                 *, nb, n_sel):
    blk = ATT_BLOCK
    j = pl.program_id(1)
    combos = [(HEAD_DIM, h, h) for h in range(N_HEADS)]
    lane = lax.broadcasted_iota(jnp.int32, (1, GW), 1)
    _att_init(q_ref[0], lane, combos, qm_ref, m_ref, l_ref, acc_ref)

    @pl.when(j == 0)
    def _():
        kmean = _dot(avg_ref[...], k_ref[0])
        for c in range(N_HEADS):
            kmc = jnp.where((lane // HEAD_DIM) == c, kmean, 0.0)
            hi = kmc.astype(BF16)
            km_ref[c * 16:c * 16 + 8, :] = hi.astype(F32)
            km_ref[c * 16 + 8:(c + 1) * 16, :] = kmc - hi.astype(F32)

    gates = _dot_nt(km_ref[...].astype(BF16), q_ref[0])
    nrow = lax.broadcasted_iota(jnp.int32, (8, blk), 0)
    past = nrow < j
    for c in range(N_HEADS):
        gate = jnp.where(past, gates[c * 16:c * 16 + 8] + gates[c * 16 + 8:(c + 1) * 16], NEG_INF)
        rank = jnp.zeros((8, blk), F32)
        for mblk in range(nb):
            gm = gate[mblk:mblk + 1, :]
            rank = rank + jnp.where(gm > gate, 1.0, jnp.where((gm == gate) & (nrow > mblk), 1.0, 0.0))
        sel_ref[c * 8:(c + 1) * 8, :] = jnp.where(past & (rank < n_sel), 1.0, 0.0)

    def body(n, carry):
        def selected(c, key0, st):
            return jnp.where(sel_ref[pl.ds(c * 8 + n, 1), :] > 0.5, st, NEG_INF)
        _att_keyblock(k_ref, vt_ref, n, combos, qm_ref, m_ref, l_ref, acc_ref, selected)
        return carry

    lax.fori_loop(0, j, body, 0)
    _att_keyblock(k_ref, vt_ref, j, combos, qm_ref, m_ref, l_ref, acc_ref, _causal_mask)
    for c in range(N_HEADS):
        ot_ref[c * HEAD_DIM:(c + 1) * HEAD_DIM, :] = _att_result(c, l_ref, acc_ref)
    o_ref[0] = ot_ref[...].T.astype(BF16)


def _att_specs(b, s, n_combos):
    blk = ATT_BLOCK
    nb = s // blk
    in_specs = [pl.BlockSpec((1, blk, GW), lambda i, j: (i, j, 0)),
                pl.BlockSpec((1, s, GW), lambda i, j: (i, 0, 0)),
                pl.BlockSpec((1, nb, GW, blk), lambda i, j: (i, 0, 0, 0))]
    out_spec = pl.BlockSpec((1, blk, GW), lambda i, j: (i, j, 0))
    scratch = [pltpu.VMEM((n_combos, blk, GW), BF16),
               pltpu.VMEM((n_combos * 8, blk), F32), pltpu.VMEM((n_combos * 8, blk), F32),
               pltpu.VMEM((n_combos * HEAD_DIM, blk), F32)]
    return in_specs, out_spec, scratch


def _moba(q, k, vt):
    b, s, _ = q.shape
    blk = ATT_BLOCK
    nb = s // blk
    assert nb <= 8
    n_sel = min(MOBA_TOPK, max(nb - 1, 1))
    avg = jnp.where(jnp.arange(s)[None, :] // blk == jnp.arange(8)[:, None], 1.0 / blk, 0.0).astype(BF16)
    in_specs, out_spec, scratch = _att_specs(b, s, N_HEADS)
    return pl.pallas_call(
        functools.partial(_moba_kernel, nb=nb, n_sel=n_sel),
        grid=(b, nb),
        in_specs=in_specs + [pl.BlockSpec((8, s), lambda i, j: (0, 0))],
        out_specs=out_spec,
        out_shape=jax.ShapeDtypeStruct((b, s, GW), BF16),
        scratch_shapes=scratch + [pltpu.VMEM((N_HEADS * 8, blk), F32), pltpu.VMEM((GW, blk), F32),
                                  pltpu.VMEM((N_HEADS * 16, GW), F32)],
        compiler_params=_params("parallel", "arbitrary"),
        name="moba",
    )(q, k, vt.reshape(b, nb, GW, blk), avg)


def _diff_kernel(q_ref, k_ref, vt_ref, lqk_ref, g_ref, o_ref, qm_ref, m_ref, l_ref, acc_ref, ot_ref,
                 *, lambda_init):
    blk = ATT_BLOCK
    j = pl.program_id(1)
    combos = [(DIFF_QK_DIM, 2 * h + i, h) for h in range(N_HEADS) for i in range(2)]
    lane = lax.broadcasted_iota(jnp.int32, (1, GW), 1)
    _att_init(q_ref[0], lane, combos, qm_ref, m_ref, l_ref, acc_ref)

    def body(n, carry):
        _att_keyblock(k_ref, vt_ref, n, combos, qm_ref, m_ref, l_ref, acc_ref, lambda c, key0, st: st)
        return carry

    lax.fori_loop(0, j, body, 0)
    _att_keyblock(k_ref, vt_ref, j, combos, qm_ref, m_ref, l_ref, acc_ref, _causal_mask)

    lqk = lqk_ref[...]
    lam = (jnp.exp(jnp.sum(lqk[0:1] * lqk[1:2], axis=1, keepdims=True))
           - jnp.exp(jnp.sum(lqk[2:3] * lqk[3:4], axis=1, keepdims=True)) + lambda_init)
    for h in range(N_HEADS):
        x = _att_result(2 * h, l_ref, acc_ref) - lam * _att_result(2 * h + 1, l_ref, acc_ref)
        ms = jnp.mean(x * x, axis=0, keepdims=True)
        ot_ref[h * HEAD_DIM:(h + 1) * HEAD_DIM, :] = x * lax.rsqrt(ms + DIFF_SUBLN_EPS)
    o_ref[0] = (ot_ref[...].T * g_ref[...] * (1.0 - lambda_init)).astype(BF16)


def _diff(q, k, vt, lqk, g_tiled, lambda_init):
    b, s, _ = q.shape
    blk = ATT_BLOCK
    nb = s // blk
    in_specs, out_spec, scratch = _att_specs(b, s, 2 * N_HEADS)
    return pl.pallas_call(
        functools.partial(_diff_kernel, lambda_init=lambda_init),
        grid=(b, nb),
        in_specs=in_specs + [pl.BlockSpec((4, 128), lambda i, j: (0, 0)),
                             pl.BlockSpec((1, GW), lambda i, j: (0, 0))],
        out_specs=out_spec,
        out_shape=jax.ShapeDtypeStruct((b, s, GW), BF16),
        scratch_shapes=scratch + [pltpu.VMEM((GW, blk), F32)],
        compiler_params=_params("parallel", "arbitrary"),
        name="diffattn",
    )(q, k, vt.reshape(b, nb, GW, blk), lqk, g_tiled)


def _sgu_kernel(pc_ref, lng_ref, lnb_ref, w_ref, bias_ref, o_ref):
    rows = pc_ref.shape[0]
    pc = pc_ref[...].astype(F32)
    u = _gelu(pc[:, :GW])
    v = _gelu(pc[:, GW:])
    mu = jnp.mean(v, axis=-1, keepdims=True)
    d = v - mu
    var = jnp.mean(d * d, axis=-1, keepdims=True)
    vn = d * lax.rsqrt(var + 1e-5) * lng_ref[...] + lnb_ref[...]
    lane = lax.broadcasted_iota(jnp.int32, (1, GW), 1)
    tr = lax.broadcasted_iota(jnp.int32, (SGU_CHUNK, SGU_CHUNK), 0)
    tc = lax.broadcasted_iota(jnp.int32, (SGU_CHUNK, SGU_CHUNK), 1)
    wm = [jnp.where(tr >= tc, w_ref[h], 0.0).astype(BF16) for h in range(N_HEADS)]
    for c in range(rows // SGU_CHUNK):
        sl = slice(c * SGU_CHUNK, (c + 1) * SGU_CHUNK)
        vc = vn[sl]
        mixed = bias_ref[...]
        for h in range(N_HEADS):
            vm = jnp.where((lane // HEAD_DIM) == h, vc, 0.0).astype(BF16)
            mixed = mixed + _dot(wm[h], vm)
        o_ref[sl, :] = (u[sl] * mixed).astype(BF16)


def _sgu(pc, lng, lnb, w, bias2d):
    t = pc.shape[0]
    tm = ROW_TILE
    const2 = lambda i: (0, 0)
    return pl.pallas_call(
        _sgu_kernel,
        grid=(t // tm,),
        in_specs=[pl.BlockSpec((tm, 2 * GW), lambda i: (i, 0)),
                  pl.BlockSpec((1, GW), const2), pl.BlockSpec((1, GW), const2),
                  pl.BlockSpec((N_HEADS, SGU_CHUNK, SGU_CHUNK), lambda i: (0, 0, 0)),
                  pl.BlockSpec((SGU_CHUNK, GW), const2)],
        out_specs=pl.BlockSpec((tm, GW), lambda i: (i, 0)),
        out_shape=jax.ShapeDtypeStruct((t, GW), BF16),
        compiler_params=_params("parallel"),
        name="sgu",
    )(pc, lng, lnb, w, bias2d)


def _rwkv_kernel(pb_ref, mu_ref, wl_ref, w0_ref, a0_ref, kk_ref, ka_ref, rk_ref, lng_ref, lnb_ref,
                 bd_ref, ltri_ref, o_ref,
                 state_ref, prev_ref, r_s, lw_s, k_s, v_s, al_s, be_s, y_s):
    rt = pb_ref.shape[1]
    c_len = RWKV_CHUNK

    @pl.when(pl.program_id(1) == 0)
    def _():
        state_ref[...] = jnp.zeros_like(state_ref)
        prev_ref[...] = jnp.zeros_like(prev_ref)

    bd = bd_ref[...]
    xt = pb_ref[0].astype(F32)
    rowid = lax.broadcasted_iota(jnp.int32, (rt, 1), 0)
    prev = jnp.where(rowid == 0, prev_ref[...], pltpu.roll(xt, 1, axis=0))
    prev_ref[...] = xt[rt - 1:rt, :]
    xs = xt + (prev - xt) * mu_ref[...]
    r = xs[:, 0:GW]
    k = xs[:, GW:2 * GW]
    v = xs[:, 2 * GW:3 * GW]
    lo = xs[:, 3 * GW:4 * GW]
    lane = lax.broadcasted_iota(jnp.int32, (1, GW), 1)
    lin = jnp.where(lane < 64, jnp.tanh(lo), jnp.where(lane < 128, lo, _sigmoid(lo)))
    z = _dot(lin.astype(BF16), wl_ref[...])
    wz = -(w0_ref[...] + z[:, 0:GW])
    softplus = jnp.maximum(wz, 0.0) + jnp.log(1.0 + jnp.exp(-jnp.abs(wz)))
    lw = -jnp.exp(-softplus - 0.5)
    a = _sigmoid(a0_ref[...] + z[:, GW:2 * GW])
    gate = z[:, 2 * GW:3 * GW]
    kk = k * kk_ref[...]
    kk = kk / jnp.maximum(jnp.sqrt(_segsum(kk * kk, bd)), 1e-12)
    k2 = k * (1.0 + (a - 1.0) * ka_ref[...])
    bonus = _segsum(r * k2 * rk_ref[...], bd) * v

    r_s[...] = r
    lw_s[...] = lw
    k_s[...] = k2
    v_s[...] = v
    al_s[...] = -kk
    be_s[...] = kk * a

    hms = [(lane // HEAD_DIM) == h for h in range(N_HEADS)]
    brow = lax.broadcasted_iota(jnp.int32, (GW, GW), 0)
    bcol = lax.broadcasted_iota(jnp.int32, (GW, GW), 1)
    strict = brow > bcol
    lower = brow >= bcol
    eye = brow == bcol
    ltri = ltri_ref[...]

    def big(x):
        return jnp.concatenate([jnp.where(hm, x, 0.0) for hm in hms], axis=0)

    def unbig(x):
        return x[0:c_len] + x[c_len:2 * c_len] + x[2 * c_len:3 * c_len] + x[3 * c_len:4 * c_len]

    def chunk(c, carry):
        sl = pl.ds(pl.multiple_of(c * c_len, c_len), c_len)
        lwc = lw_s[sl, :]
        h3 = _split3(lwc)
        cs = _dot(ltri, h3[0]) + _dot(ltri, h3[1]) + _dot(ltri, h3[2])
        cs_end = cs[c_len - 1:c_len, :]
        e_neg = jnp.exp(-cs)
        e_rem = jnp.exp(cs_end - cs)
        al, be, kc, rc, vc = al_s[sl, :], be_s[sl, :], k_s[sl, :], r_s[sl, :], v_s[sl, :]
        r_t = rc * jnp.exp(cs)
        a_b = big(al * jnp.exp(cs - lwc)).astype(BF16)
        r_b = big(r_t).astype(BF16)
        bi_b = big(be * e_neg).astype(BF16)
        ki_b = big(kc * e_neg).astype(BF16)
        v_b = big(vc).astype(BF16)
        a_ab = jnp.where(strict, _dot_nt(a_b, bi_b), 0.0)
        a_ak = jnp.where(strict, _dot_nt(a_b, ki_b), 0.0)
        a_rb = jnp.where(lower, _dot_nt(r_b, bi_b), 0.0).astype(BF16)
        a_rk = jnp.where(lower, _dot_nt(r_b, ki_b), 0.0).astype(BF16)
        p = a_ab
        tinv = jnp.where(eye, 1.0, a_ab)
        for _ in range(5):
            pb = p.astype(BF16)
            p = _dot(pb, pb)
            tinv = tinv + _dot(tinv.astype(BF16), p.astype(BF16))
        tb = tinv.astype(BF16)
        av = _dot(a_ak.astype(BF16), v_b)
        p1 = _dot(tb, a_b).astype(BF16)
        p2 = _dot(tb, av.astype(BF16)).astype(BF16)
        q_tok = r_t + unbig(_dot(a_rb, p1))
        z_tok = unbig(_dot(a_rb, p2) + _dot(a_rk, v_b))
        bd_t = big(be * e_rem).T.astype(BF16)
        kd_t = big(kc * e_rem).T.astype(BF16)
        m_c = jnp.where(eye, jnp.exp(cs_end), 0.0) + _dot(bd_t, p1)
        n_c = _dot(bd_t, p2) + _dot(kd_t, v_b)
        skb = state_ref[...].astype(BF16)
        y_s[sl, :] = _dot(q_tok.astype(BF16), skb) + z_tok
        state_ref[...] = _dot(m_c.astype(BF16), skb) + n_c
        return carry

    lax.fori_loop(0, rt // c_len, chunk, 0)

    y = y_s[...]
    mean = _segsum(y, bd) * (1.0 / HEAD_DIM)
    d = y - mean
    var = _segsum(d * d, bd) * (1.0 / HEAD_DIM)
    yn = d * lax.rsqrt(var + RWKV_LN_EPS) * lng_ref[...] + lnb_ref[...]
    o_ref[0] = ((yn + bonus) * gate).astype(BF16)


def _rwkv(pb, mu, wl, w0, a0, kk, ka, rk, lng, lnb, bd, ltri):
    b, s, _ = pb.shape
    rt = RWKV_ROWS
    c2 = lambda i, j: (0, 0)
    vec = pl.BlockSpec((1, GW), c2)
    return pl.pallas_call(
        _rwkv_kernel,
        grid=(b, s // rt),
        in_specs=[pl.BlockSpec((1, rt, 4 * GW), lambda i, j: (i, j, 0)),
                  pl.BlockSpec((1, 4 * GW), c2),
                  pl.BlockSpec((GW, 3 * GW), c2),
                  vec, vec, vec, vec, vec, vec, vec,
                  pl.BlockSpec((GW, GW), c2),
                  pl.BlockSpec((RWKV_CHUNK, RWKV_CHUNK), c2)],
        out_specs=pl.BlockSpec((1, rt, GW), lambda i, j: (i, j, 0)),
        out_shape=jax.ShapeDtypeStruct((b, s, GW), BF16),
        scratch_shapes=[pltpu.VMEM((GW, GW), F32), pltpu.VMEM((1, 4 * GW), F32)]
                       + [pltpu.VMEM((rt, GW), F32) for _ in range(7)],
        compiler_params=_params("parallel", "arbitrary"),
        name="rwkv7",
    )(pb, mu, wl, w0, a0, kk, ka, rk, lng, lnb, bd, ltri)


def _outproj_kernel(oa_ref, ob_ref, oc_ref, od_ref, x_ref, w_ref, g_ref, wrh_ref, wrl_ref, br_ref,
                    h_o, hn_o, eid_o, wts_o):
    mixed = jnp.concatenate([oa_ref[...], ob_ref[...], oc_ref[...], od_ref[...]], axis=1)
    h = x_ref[...] + _dot(mixed, w_ref[...])
    h_o[...] = h
    ms = jnp.mean(h * h, axis=-1, keepdims=True)
    hn = h * lax.rsqrt(ms + RMS_EPS) * g_ref[...]
    _store_row_tiles(hn_o, hn)
    hi = hn.astype(BF16)
    lo = (hn - hi.astype(F32)).astype(BF16)
    wrh = wrh_ref[...]
    logits = _dot(hi, wrh) + _dot(lo, wrh) + _dot(hi, wrl_ref[...]) + br_ref[...]
    tm = logits.shape[0]
    lane = lax.broadcasted_iota(jnp.int32, (tm, 128), 1)
    lanef = lane.astype(F32)
    big_idx = 1e9

    def rmax(t):
        return jnp.max(t, axis=1, keepdims=True)

    def rmin(t):
        return jnp.min(t, axis=1, keepdims=True)

    def rsum(t):
        return jnp.sum(t, axis=1, keepdims=True)

    gmask = lane < N_EXPERT_GROUPS
    gl = jnp.where(gmask, logits, NEG_INF)
    gmax = rmax(gl)
    gidx = rmin(jnp.where(gl == gmax, lanef, big_idx))
    gp = 1.0 / rsum(jnp.where(gmask, jnp.exp(logits - gmax), 0.0))
    e_lo = N_EXPERT_GROUPS + gidx * EXPERTS_PER_GROUP
    emask = (lanef >= e_lo) & (lanef < e_lo + EXPERTS_PER_GROUP)
    el = jnp.where(emask, logits, NEG_INF)
    emax = rmax(el)
    esum = rsum(jnp.where(emask, jnp.exp(logits - emax), 0.0))
    i1 = rmin(jnp.where(el == emax, lanef, big_idx))
    el2 = jnp.where(lanef == i1, NEG_INF, el)
    m2 = rmax(el2)
    i2 = rmin(jnp.where(el2 == m2, lanef, big_idx))
    p1 = 1.0 / esum
    p2 = jnp.exp(m2 - emax) / esum
    w1 = gp * (p1 / (p1 + p2))
    w2 = gp * (p2 / (p1 + p2))
    eid = jnp.where(lane == 0, i1 - N_EXPERT_GROUPS, jnp.where(lane == 1, i2 - N_EXPERT_GROUPS, 0.0))
    eid_o[...] = eid.astype(jnp.int32)
    wts_o[...] = jnp.where(lane == 0, w1, jnp.where(lane == 1, w2, 0.0))


def _outproj(oa, ob, oc, od, x2, w_out, g2, wr_hi, wr_lo, br):
    t = x2.shape[0]
    tm = ROW_TILE
    row = lambda i: (i, 0)
    const = lambda i: (0, 0)
    mix = pl.BlockSpec((tm, GW), row)
    return pl.pallas_call(
        _outproj_kernel,
        grid=(t // tm,),
        in_specs=[mix, mix, mix, mix,
                  pl.BlockSpec((tm, D_MODEL), row),
                  pl.BlockSpec((D_MODEL, D_MODEL), const),
                  pl.BlockSpec((1, D_MODEL), const),
                  pl.BlockSpec((D_MODEL, 128), const), pl.BlockSpec((D_MODEL, 128), const),
                  pl.BlockSpec((1, 128), const)],
        out_specs=[pl.BlockSpec((tm, D_MODEL), row), pl.BlockSpec((tm * ROW_TILE_ROWS, 128), row),
                   pl.BlockSpec((tm, 128), row), pl.BlockSpec((tm, 128), row)],
        out_shape=[jax.ShapeDtypeStruct((t, D_MODEL), F32), jax.ShapeDtypeStruct((t * ROW_TILE_ROWS, 128), F32),
                   jax.ShapeDtypeStruct((t, 128), jnp.int32), jax.ShapeDtypeStruct((t, 128), F32)],
        compiler_params=_params("parallel"),
        name="outproj_router",
    )(oa, ob, oc, od, x2, w_out, g2, wr_hi, wr_lo, br)


def _expert_kernel(blk_e_ref, blk_nv_ref, tok_ref, tokn_ref, dst_ref, hn_hbm, wg_ref, wu_ref, wd_ref, y_hbm,
                   xbuf, ybuf, wgb, wub, wdb, gsem, ssem):
    i = pl.program_id(0)
    nblk = pl.num_programs(0)
    slot = i % 2
    other = 1 - slot
    tr = ROW_TILE_ROWS
    full = xbuf.shape[1] // tr
    nv = blk_nv_ref[i]
    nv_next = jnp.where(i + 1 < nblk, blk_nv_ref[jnp.minimum(i + 1, nblk - 1)], 0)
    nv_prev = jnp.where(i >= 1, blk_nv_ref[jnp.maximum(i - 1, 0)], 0)
    nv_prev2 = jnp.where(i >= 2, blk_nv_ref[jnp.maximum(i - 2, 0)], 0)

    def tile_rows(start):
        return pl.ds(start if isinstance(start, int) else pl.multiple_of(start, tr), tr)

    def row_in(idx_ref, s, r):
        return pltpu.make_async_copy(hn_hbm.at[tile_rows(idx_ref[0, 0, r])], xbuf.at[s, tile_rows(r * tr)], gsem.at[s])

    def row_out(s, r):
        return pltpu.make_async_copy(ybuf.at[s, tile_rows(r * tr)], y_hbm.at[tile_rows(dst_ref[0, 0, r])], ssem.at[s])

    def for_rows(n, fn):
        def body(r, carry):
            fn(r)
            return carry
        lax.fori_loop(0, n, body, 0)

    def ffn(issue_next_gather):
        xb = _load_row_tiles(xbuf.at[slot], full).astype(BF16)
        issue_next_gather()
        hg = _dot(xb, wgb[...])
        hu = _dot(xb, wub[...])
        hdn = (hg * _sigmoid(hg) * hu).astype(BF16)
        _store_row_tiles(ybuf.at[slot], _dot(hdn, wdb[...]))

    def wait_rows(n, block_copy, row_copy):
        @pl.when(n == full)
        def _():
            block_copy.wait()

        @pl.when(n < full)
        def _():
            lax.fori_loop(0, n, lambda r, carry: (row_copy.wait(), carry)[1], 0)

    def wait_gather(n, s):
        wait_rows(n, pltpu.make_async_copy(hn_hbm.at[pl.ds(0, full * tr)], xbuf.at[s], gsem.at[s]),
                  pltpu.make_async_copy(hn_hbm.at[pl.ds(0, tr)], xbuf.at[s, pl.ds(0, tr)], gsem.at[s]))

    def wait_scatter(n, s):
        wait_rows(n, pltpu.make_async_copy(ybuf.at[s], y_hbm.at[pl.ds(0, full * tr)], ssem.at[s]),
                  pltpu.make_async_copy(ybuf.at[s, pl.ds(0, tr)], y_hbm.at[pl.ds(0, tr)], ssem.at[s]))

    @pl.when(i == 0)
    def _():
        xbuf[...] = jnp.zeros_like(xbuf)
        for_rows(nv, lambda r: row_in(tok_ref, slot, r).start())

    wait_gather(nv, slot)
    wait_scatter(nv_prev2, slot)

    @pl.when((i == 0) | (blk_e_ref[i] != blk_e_ref[jnp.maximum(i - 1, 0)]))
    def _():
        wgb[...] = wg_ref[0, 0].astype(BF16)
        wub[...] = wu_ref[0, 0].astype(BF16)
        wdb[...] = wd_ref[0, 0].astype(BF16)

    fast = (nv == full) & (nv_next == full)

    @pl.when(fast)
    def _():
        def issue_next_gather():
            for r in range(full):
                row_in(tokn_ref, other, r).start()
        ffn(issue_next_gather)
        for r in range(full):
            row_out(slot, r).start()

    @pl.when(jnp.logical_not(fast))
    def _():
        for_rows(nv_next, lambda r: row_in(tokn_ref, other, r).start())

        @pl.when(nv > 0)
        def _():
            ffn(lambda: None)
            for_rows(nv, lambda r: row_out(slot, r).start())

    @pl.when(i == nblk - 1)
    def _():
        wait_scatter(nv_prev, other)
        wait_scatter(nv, slot)


def _experts(blk_e, blk_nv, row_tok, row_dst, hn, wg, wu, wd, layer, m_rows):
    nblk = blk_e.shape[0]
    r = MOE_ROWS
    grid_spec = pltpu.PrefetchScalarGridSpec(
        num_scalar_prefetch=2,
        grid=(nblk,),
        in_specs=[pl.BlockSpec((1, 1, r), lambda i, be, nv: (i, 0, 0), memory_space=pltpu.SMEM),
                  pl.BlockSpec((1, 1, r), lambda i, be, nv: (jnp.minimum(i + 1, nblk - 1), 0, 0),
                               memory_space=pltpu.SMEM),
                  pl.BlockSpec((1, 1, r), lambda i, be, nv: (i, 0, 0), memory_space=pltpu.SMEM),
                  pl.BlockSpec(memory_space=pl.ANY),
                  pl.BlockSpec((1, 1, D_MODEL, EXPERT_FF), lambda i, be, nv: (layer, be[i], 0, 0)),
                  pl.BlockSpec((1, 1, D_MODEL, EXPERT_FF), lambda i, be, nv: (layer, be[i], 0, 0)),
                  pl.BlockSpec((1, 1, EXPERT_FF, D_MODEL), lambda i, be, nv: (layer, be[i], 0, 0))],
        out_specs=pl.BlockSpec(memory_space=pl.ANY),
        scratch_shapes=[pltpu.VMEM((2, r * ROW_TILE_ROWS, 128), F32), pltpu.VMEM((2, r * ROW_TILE_ROWS, 128), F32),
                        pltpu.VMEM((D_MODEL, EXPERT_FF), BF16), pltpu.VMEM((D_MODEL, EXPERT_FF), BF16),
                        pltpu.VMEM((EXPERT_FF, D_MODEL), BF16),
                        pltpu.SemaphoreType.DMA((2,)), pltpu.SemaphoreType.DMA((2,))],
    )
    return pl.pallas_call(
        _expert_kernel,
        grid_spec=grid_spec,
        out_shape=jax.ShapeDtypeStruct((m_rows * ROW_TILE_ROWS, 128), F32),
        compiler_params=_params("arbitrary"),
        name="experts",
    )(blk_e, blk_nv, row_tok, row_tok, row_dst, hn, wg, wu, wd)


def _dispatch(eid, t):
    r = MOE_ROWS
    m = 2 * t
    nblk = m // r + N_EXPERTS
    experts = jnp.arange(N_EXPERTS, dtype=jnp.int32)
    e_flat = eid[:, :2].reshape(m)
    counts = jnp.sum((e_flat[:, None] == experts[None, :]).astype(jnp.int32), axis=0)
    n_pad = (r - counts % r) % r
    pad_keys = jnp.where(jnp.arange(r, dtype=jnp.int32)[None, :] < n_pad[:, None], experts[:, None], N_EXPERTS)
    keys = jnp.concatenate([e_flat, pad_keys.reshape(N_EXPERTS * r)])
    payload = jnp.concatenate([jnp.arange(m, dtype=jnp.int32), jnp.full((N_EXPERTS * r,), -1, jnp.int32)])
    keys_s, a = lax.sort((keys, payload), num_keys=1, is_stable=True)
    valid = a >= 0
    row_tok = jnp.where(valid, a // 2, 0) * ROW_TILE_ROWS
    row_dst = jnp.where(valid, (a % 2) * t + a // 2, 0) * ROW_TILE_ROWS
    blk_e = jnp.minimum(keys_s.reshape(nblk, r)[:, 0], N_EXPERTS - 1)
    blk_nv = jnp.sum(valid.reshape(nblk, r).astype(jnp.int32), axis=1)
    return blk_e, blk_nv, row_tok.reshape(nblk, 1, r), row_dst.reshape(nblk, 1, r)


def _combine_kernel(h_ref, y0_ref, y1_ref, wts_ref, fg_ref, o_ref, *, final):
    w = wts_ref[...]
    rows = h_ref.shape[0]
    x = (h_ref[...] + w[:, 0:1] * _load_row_tiles(y0_ref.at[0], rows)
         + w[:, 1:2] * _load_row_tiles(y1_ref.at[0], rows))
    if final:
        ms = jnp.mean(x * x, axis=-1, keepdims=True)
        x = x * lax.rsqrt(ms + RMS_EPS) * fg_ref[...]
    o_ref[...] = x


def _combine(h, y, wts, fg, final):
    t = h.shape[0]
    tm = ROW_TILE
    row = lambda i: (i, 0)
    return pl.pallas_call(
        functools.partial(_combine_kernel, final=final),
        grid=(t // tm,),
        in_specs=[pl.BlockSpec((tm, D_MODEL), row),
                  pl.BlockSpec((1, tm * ROW_TILE_ROWS, 128), lambda i: (0, i, 0)),
                  pl.BlockSpec((1, tm * ROW_TILE_ROWS, 128), lambda i: (1, i, 0)),
                  pl.BlockSpec((tm, 128), row),
                  pl.BlockSpec((1, D_MODEL), lambda i: (0, 0))],
        out_specs=pl.BlockSpec((tm, D_MODEL), row),
        out_shape=jax.ShapeDtypeStruct((t, D_MODEL), F32),
        compiler_params=_params("parallel"),
        name="combine",
    )(h, y, y, wts, fg)


def _rope_tables(seq, head_width, rot_dim):
    half = rot_dim // 2
    inv = ROPE_THETA ** (-jnp.arange(0, rot_dim, 2, dtype=F32) / rot_dim)
    ang = jnp.arange(seq, dtype=F32)[:, None] * inv[None, :]
    cos, sin = jnp.cos(ang), jnp.sin(ang)
    d = np.arange(GW) % head_width
    idx = d % half
    cos_t = jnp.where((d < rot_dim)[None, :], cos[:, idx], 1.0)
    sin_t = jnp.where((d < half)[None, :], -sin[:, idx], jnp.where((d < rot_dim)[None, :], sin[:, idx], 0.0))
    return cos_t.astype(F32), sin_t.astype(F32)


def _swap_cols(w, head_width, rot_dim):
    half = rot_dim // 2
    c = np.arange(GW)
    d = c % head_width
    partner = np.where(d < half, c + half, np.where(d < rot_dim, c - half, c))
    keep = (d < rot_dim).astype(np.float32)
    return w[:, partner] * keep[None, :]


def _layer_weights(l, w_in, rwkv_mu, rwkv_w2, rwkv_a2, rwkv_g2, sgu_w, sgu_b, diff_l, diff_subln_g,
                   router_group_w, router_group_b, router_expert_w, router_expert_b):
    wi = w_in[l]
    a0, b0, c0, d0 = 0, 3 * GW, 3 * GW + 4 * GW, 3 * GW + 4 * GW + 2 * GW
    qa, ka, va = wi[:, a0:a0 + GW], wi[:, a0 + GW:a0 + 2 * GW], wi[:, a0 + 2 * GW:a0 + 3 * GW]
    bsl = [(0, 256), (320, 576), (576, 832), (256, 320), (832, 896), (896, 1024)]
    wb = jnp.concatenate([wi[:, b0 + s:b0 + e] for s, e in bsl], axis=1)
    mu = jnp.concatenate([rwkv_mu[l][s:e] for s, e in bsl])[None, :]
    wc = wi[:, c0:c0 + 2 * GW]
    qd, kd, vd = wi[:, d0:d0 + GW], wi[:, d0 + GW:d0 + 2 * GW], wi[:, d0 + 2 * GW:d0 + 3 * GW]
    w_all = jnp.concatenate(
        [qa, _swap_cols(qa, HEAD_DIM, 16), ka, _swap_cols(ka, HEAD_DIM, 16), wb, wc,
         qd, _swap_cols(qd, DIFF_QK_DIM, 8), kd, _swap_cols(kd, DIFF_QK_DIM, 8)], axis=1).astype(BF16)
    wvt = jnp.concatenate([va, vd], axis=1).T.astype(BF16)
    wl = jnp.zeros((GW, 3 * GW), F32)
    wl = wl.at[0:64, 0:GW].set(rwkv_w2[l]).at[64:128, GW:2 * GW].set(rwkv_a2[l]).at[128:256, 2 * GW:].set(rwkv_g2[l])
    bias2d = jnp.repeat(sgu_b[l].T, HEAD_DIM, axis=1)
    lqk = jnp.zeros((4, 128), F32).at[:, :DIFF_QK_DIM].set(jnp.stack([d[l] for d in diff_l]))
    wr = jnp.zeros((D_MODEL, 128), F32).at[:, :N_EXPERT_GROUPS].set(router_group_w[l])
    wr = wr.at[:, N_EXPERT_GROUPS:N_EXPERT_GROUPS + N_EXPERTS].set(router_expert_w[l])
    wr_hi = wr.astype(BF16)
    wr_lo = (wr - wr_hi.astype(F32)).astype(BF16)
    br = jnp.zeros((1, 128), F32).at[0, :N_EXPERT_GROUPS].set(router_group_b[l])
    br = br.at[0, N_EXPERT_GROUPS:N_EXPERT_GROUPS + N_EXPERTS].set(router_expert_b[l])
    return dict(w_all=w_all, wvt=wvt, mu=mu, wl=wl.astype(BF16), bias2d=bias2d, lqk=lqk,
                subln=jnp.tile(diff_subln_g[l], N_HEADS)[None, :], wr_hi=wr_hi, wr_lo=wr_lo, br=br)


def kernel(x, norm1_g, w_in, rwkv_mu, rwkv_w0, rwkv_w2, rwkv_a0, rwkv_a2, rwkv_g2, rwkv_kk, rwkv_ka, rwkv_rk, rwkv_ln_g, rwkv_ln_b, sgu_ln_g, sgu_ln_b, sgu_w, sgu_b, diff_lq1, diff_lk1, diff_lq2, diff_lk2, diff_subln_g, w_out, norm2_g, router_group_w, router_group_b, router_expert_w, router_expert_b, moe_w_gate, moe_w_up, moe_w_down, final_g):
    b, s, d = x.shape
    depth = w_in.shape[0]
    t = b * s
    ca, sa = _rope_tables(s, HEAD_DIM, 16)
    cd, sd = _rope_tables(s, DIFF_QK_DIM, 8)
    seg = np.arange(GW) // HEAD_DIM
    bd = jnp.asarray(seg[:, None] == seg[None, :], BF16)
    ltri = jnp.asarray(np.tril(np.ones((RWKV_CHUNK, RWKV_CHUNK))), BF16)
    row = lambda p: p[None, :]
    x2 = x.reshape(t, d)
    for l in range(depth):
        lw = _layer_weights(l, w_in, rwkv_mu, rwkv_w2, rwkv_a2, rwkv_g2, sgu_w, sgu_b,
                            (diff_lq1, diff_lk1, diff_lq2, diff_lk2), diff_subln_g,
                            router_group_w, router_group_b, router_expert_w, router_expert_b)
        qa, ka, va, pb, pc, qd, kd, vd = _inproj(x2, row(norm1_g[l]), lw["w_all"], lw["wvt"], ca, sa, cd, sd, s)
        seq3 = lambda z: z.reshape(b, s, z.shape[-1])
        out_a = _moba(seq3(qa), seq3(ka), va)
        out_b = _rwkv(seq3(pb), lw["mu"], lw["wl"], row(rwkv_w0[l]), row(rwkv_a0[l]), row(rwkv_kk[l]),
                      row(rwkv_ka[l]), row(rwkv_rk[l].reshape(GW)), row(rwkv_ln_g[l]), row(rwkv_ln_b[l]), bd, ltri)
        out_c = _sgu(pc, row(sgu_ln_g[l]), row(sgu_ln_b[l]), sgu_w[l], lw["bias2d"])
        lambda_init = 0.8 - 0.6 * math.exp(-0.3 * l)
        out_d = _diff(seq3(qd), seq3(kd), vd, lw["lqk"], lw["subln"], lambda_init)
        h, hn, eid, wts = _outproj(out_a.reshape(t, GW), out_b.reshape(t, GW), out_c, out_d.reshape(t, GW),
                                   x2, w_out[l].astype(BF16), row(norm2_g[l]), lw["wr_hi"], lw["wr_lo"], lw["br"])
        blk_e, blk_nv, row_tok, row_dst = _dispatch(eid, t)
        y = _experts(blk_e, blk_nv, row_tok, row_dst, hn, moe_w_gate, moe_w_up, moe_w_down, l, 2 * t)
        x2 = _combine(h, y.reshape(2, t * ROW_TILE_ROWS, 128), wts, row(final_g), l == depth - 1)
    return x2.reshape(b, s, d)
```

```python
import functools
import math

import numpy as np
import jax
import jax.numpy as jnp
from jax import lax
from jax.experimental import pallas as pl
from jax.experimental.pallas import tpu as pltpu

F32 = jnp.float32
BF16 = jnp.bfloat16

D_MODEL = 1024
HEAD_DIM = 64
N_HEADS = 4
GW = N_HEADS * HEAD_DIM
ROPE_THETA = 500000.0
RMS_EPS = 1e-6

ATT_BLOCK = 256
ATT_SUB = 256
LOG2E = 1.4426950408889634
MOBA_TOPK = 3
RWKV_LN_EPS = 64e-5
RWKV_CHUNK = 64
RWKV_ROWS = 256
RWKV_INTERLEAVE = 4
SGU_CHUNK = 128
DIFF_QK_DIM = 32
DIFF_SUBLN_EPS = 1e-5

N_EXPERT_GROUPS = 4
EXPERTS_PER_GROUP = 8
N_EXPERTS = 32
EXPERT_FF = 512
MOE_ROWS = 256

ROW_TILE = 512
VMEM_LIMIT = 56 * 1024 * 1024

NEG_INF = float("-inf")


def _dot(a, b):
    return jnp.dot(a, b, preferred_element_type=F32)


def _dot_nt(a, b):
    return lax.dot_general(a, b, (((1,), (1,)), ((), ())), preferred_element_type=F32)


def _split3(x):
    hi = x.astype(BF16)
    r1 = x - hi.astype(F32)
    mid = r1.astype(BF16)
    lo = (r1 - mid.astype(F32)).astype(BF16)
    return hi, mid, lo


def _segsum(x, bd):
    hi, mid, lo = _split3(x)
    return _dot(hi, bd) + _dot(mid, bd) + _dot(lo, bd)


def _sigmoid(x):
    return 1.0 / (1.0 + jnp.exp(-x))


def _gelu(x):
    return 0.5 * x * (1.0 + lax.erf(x * (1.0 / math.sqrt(2.0))))


ROW_TILE_ROWS = D_MODEL // 128


def _store_row_tiles(ref, x):
    rows = x.shape[0]
    for k in range(ROW_TILE_ROWS):
        ref[pl.ds(k, rows, stride=ROW_TILE_ROWS), :] = x[:, k * 128:(k + 1) * 128]


def _load_row_tiles(ref, rows):
    return jnp.concatenate([ref[pl.ds(k, rows, stride=ROW_TILE_ROWS), :] for k in range(ROW_TILE_ROWS)], axis=1)


def _params(*sem):
    return pltpu.CompilerParams(dimension_semantics=sem, vmem_limit_bytes=VMEM_LIMIT)


def _inproj_kernel(x_ref, g_ref, w_ref, wvt_ref, ca_ref, sa_ref, cd_ref, sd_ref,
                   qa_o, ka_o, vta_o, pb_o, pc_o, qd_o, kd_o, vtd_o):
    x = x_ref[...]
    ms = jnp.mean(x * x, axis=-1, keepdims=True)
    xn = (x * lax.rsqrt(ms + RMS_EPS) * g_ref[...]).astype(BF16)

    def proj(c0, n=1):
        return _dot(xn, w_ref[:, c0 * GW:(c0 + n) * GW])

    ca, sa = ca_ref[...], sa_ref[...]
    qa_o[...] = ((proj(0) * ca + proj(1) * sa) * (HEAD_DIM ** -0.5 * LOG2E)).astype(BF16)
    ka_o[...] = (proj(2) * ca + proj(3) * sa).astype(BF16)
    pb_o[...] = proj(4, 4).astype(BF16)
    pc_o[...] = proj(8, 2).astype(BF16)
    cd, sd = cd_ref[...], sd_ref[...]
    qd_o[...] = ((proj(10) * cd + proj(11) * sd) * (DIFF_QK_DIM ** -0.5 * LOG2E)).astype(BF16)
    kd_o[...] = (proj(12) * cd + proj(13) * sd).astype(BF16)
    vt = _dot_nt(wvt_ref[...], xn)
    for kb in range(x.shape[0] // ATT_BLOCK):
        cols = slice(kb * ATT_BLOCK, (kb + 1) * ATT_BLOCK)
        vta_o[kb] = vt[0:GW, cols].astype(BF16)
        vtd_o[kb] = vt[GW:2 * GW, cols].astype(BF16)


def _inproj(x2, g, w_all, wvt, ca, sa, cd, sd, seq):
    t = x2.shape[0]
    tm = min(ROW_TILE, seq)
    per_seq = seq // tm
    kbs = tm // ATT_BLOCK
    row = lambda i: (i, 0)
    tab = lambda i: (i % per_seq, 0)
    const = lambda i: (0, 0)
    vt_spec = pl.BlockSpec((kbs, GW, ATT_BLOCK), lambda i: (i, 0, 0))
    vt_shape = jax.ShapeDtypeStruct((t // ATT_BLOCK, GW, ATT_BLOCK), BF16)
    rows = lambda w: (pl.BlockSpec((tm, w), row), jax.ShapeDtypeStruct((t, w), BF16))
    outs = [rows(GW), rows(GW), (vt_spec, vt_shape), rows(4 * GW), rows(2 * GW), rows(GW), rows(GW),
            (vt_spec, vt_shape)]
    return pl.pallas_call(
        _inproj_kernel,
        grid=(t // tm,),
        in_specs=[pl.BlockSpec((tm, D_MODEL), row),
                  pl.BlockSpec((1, D_MODEL), const),
                  pl.BlockSpec((D_MODEL, 14 * GW), const),
                  pl.BlockSpec((2 * GW, D_MODEL), const),
                  pl.BlockSpec((tm, GW), tab), pl.BlockSpec((tm, GW), tab),
                  pl.BlockSpec((tm, GW), tab), pl.BlockSpec((tm, GW), tab)],
        out_specs=[o[0] for o in outs],
        out_shape=[o[1] for o in outs],
        compiler_params=_params("parallel"),
        name="inproj",
    )(x2, g, w_all, wvt, ca, sa, cd, sd)


M_INIT = -1e30


def _att_init(q, lane, combos, qm_ref, m_ref, l_ref, acc_ref):
    for c, (width, idx, _) in enumerate(combos):
        qm_ref[c] = jnp.where((lane // width) == idx, q, jnp.zeros_like(q))
    m_ref[...] = jnp.full_like(m_ref, M_INIT)
    l_ref[...] = jnp.zeros_like(l_ref)
    acc_ref[...] = jnp.zeros_like(acc_ref)


def _att_update(c, st, vt_sub, m_ref, l_ref, acc_ref):
    ms = slice(c * 8, c * 8 + 1)
    rows = slice(c * HEAD_DIM, (c + 1) * HEAD_DIM)
    m_old = m_ref[ms, :]
    m_new = jnp.maximum(m_old, jnp.max(st, axis=0, keepdims=True))
    alpha = jnp.exp2(m_old - m_new)
    pt = jnp.exp2(st - m_new)
    l_ref[ms, :] = alpha * l_ref[ms, :] + jnp.sum(pt, axis=0, keepdims=True)
    acc_ref[rows, :] = alpha * acc_ref[rows, :] + _dot(vt_sub, pt.astype(BF16))
    m_ref[ms, :] = m_new


def _att_keyblock(k_ref, vt_ref, n, combos, qm_ref, m_ref, l_ref, acc_ref, mask_fn):
    steps = [(piece * ATT_SUB, c) for piece in range(ATT_BLOCK // ATT_SUB) for c in range(len(combos))]

    def scores(key0, c):
        kb = k_ref[0, pl.ds(pl.multiple_of(n * ATT_BLOCK + key0, ATT_SUB), ATT_SUB), :]
        return _dot_nt(kb, qm_ref[c])

    st_next = scores(*steps[0])
    for i, (key0, c) in enumerate(steps):
        st = st_next
        if i + 1 < len(steps):
            st_next = scores(*steps[i + 1])
        head = combos[c][2]
        vt_sub = vt_ref[0, n, head * HEAD_DIM:(head + 1) * HEAD_DIM, key0:key0 + ATT_SUB]
        _att_update(c, mask_fn(c, key0, st), vt_sub, m_ref, l_ref, acc_ref)


def _causal_mask(c, key0, st):
    krow = key0 + lax.broadcasted_iota(jnp.int32, st.shape, 0)
    qcol = lax.broadcasted_iota(jnp.int32, st.shape, 1)
    return jnp.where(krow <= qcol, st, NEG_INF)


def _att_result(c, l_ref, acc_ref):
    return acc_ref[c * HEAD_DIM:(c + 1) * HEAD_DIM, :] / l_ref[c * 8:c * 8 + 1, :]


def _moba_kernel(q_ref, k_ref, vt_ref, avg_ref, o_ref, qm_ref, m_ref, l_ref, acc_ref, sel_ref, ot_ref, km_ref,
                 *, nb, n_sel):
    blk = ATT_BLOCK
    j = pl.program_id(1)
    combos = [(HEAD_DIM, h, h) for h in range(N_HEADS)]
    lane = lax.broadcasted_iota(jnp.int32, (1, GW), 1)
    _att_init(q_ref[0], lane, combos, qm_ref, m_ref, l_ref, acc_ref)

    @pl.when(j == 0)
    def _():
        kmean = _dot(avg_ref[...], k_ref[0])
        for c in range(N_HEADS):
            kmc = jnp.where((lane // HEAD_DIM) == c, kmean, 0.0)
            hi = kmc.astype(BF16)
            km_ref[c * 16:c * 16 + 8, :] = hi.astype(F32)
            km_ref[c * 16 + 8:(c + 1) * 16, :] = kmc - hi.astype(F32)

    gates = _dot_nt(km_ref[...].astype(BF16), q_ref[0])
    nrow = lax.broadcasted_iota(jnp.int32, (8, blk), 0)
    past = nrow < j
    for c in range(N_HEADS):
        gate = jnp.where(past, gates[c * 16:c * 16 + 8] + gates[c * 16 + 8:(c + 1) * 16], NEG_INF)
        rank = jnp.zeros((8, blk), F32)
        for mblk in range(nb):
            gm = gate[mblk:mblk + 1, :]
            rank = rank + jnp.where(gm > gate, 1.0, jnp.where((gm == gate) & (nrow > mblk), 1.0, 0.0))
        sel_ref[c * 8:(c + 1) * 8, :] = jnp.where(past & (rank < n_sel), 1.0, 0.0)

    def body(n, carry):
        def selected(c, key0, st):
            return jnp.where(sel_ref[pl.ds(c * 8 + n, 1), :] > 0.5, st, NEG_INF)
        _att_keyblock(k_ref, vt_ref, n, combos, qm_ref, m_ref, l_ref, acc_ref, selected)
        return carry

    lax.fori_loop(0, j, body, 0)
    _att_keyblock(k_ref, vt_ref, j, combos, qm_ref, m_ref, l_ref, acc_ref, _causal_mask)
    for c in range(N_HEADS):
        ot_ref[c * HEAD_DIM:(c + 1) * HEAD_DIM, :] = _att_result(c, l_ref, acc_ref)
    o_ref[0] = ot_ref[...].T.astype(BF16)


def _att_specs(b, s, n_combos):
    blk = ATT_BLOCK
    nb = s // blk
    in_specs = [pl.BlockSpec((1, blk, GW), lambda i, j: (i, j, 0)),
                pl.BlockSpec((1, s, GW), lambda i, j: (i, 0, 0)),
                pl.BlockSpec((1, nb, GW, blk), lambda i, j: (i, 0, 0, 0))]
    out_spec = pl.BlockSpec((1, blk, GW), lambda i, j: (i, j, 0))
    scratch = [pltpu.VMEM((n_combos, blk, GW), BF16),
               pltpu.VMEM((n_combos * 8, blk), F32), pltpu.VMEM((n_combos * 8, blk), F32),
               pltpu.VMEM((n_combos * HEAD_DIM, blk), F32)]
    return in_specs, out_spec, scratch


def _moba(q, k, vt):
    b, s, _ = q.shape
    blk = ATT_BLOCK
    nb = s // blk
    assert nb <= 8
    n_sel = min(MOBA_TOPK, max(nb - 1, 1))
    avg = jnp.where(jnp.arange(s)[None, :] // blk == jnp.arange(8)[:, None], 1.0 / blk, 0.0).astype(BF16)
    in_specs, out_spec, scratch = _att_specs(b, s, N_HEADS)
    return pl.pallas_call(
        functools.partial(_moba_kernel, nb=nb, n_sel=n_sel),
        grid=(b, nb),
        in_specs=in_specs + [pl.BlockSpec((8, s), lambda i, j: (0, 0))],
        out_specs=out_spec,
        out_shape=jax.ShapeDtypeStruct((b, s, GW), BF16),
        scratch_shapes=scratch + [pltpu.VMEM((N_HEADS * 8, blk), F32), pltpu.VMEM((GW, blk), F32),
                                  pltpu.VMEM((N_HEADS * 16, GW), F32)],
        compiler_params=_params("parallel", "arbitrary"),
        name="moba",
    )(q, k, vt.reshape(b, nb, GW, blk), avg)


def _diff_kernel(q_ref, k_ref, vt_ref, lqk_ref, g_ref, o_ref, qm_ref, m_ref, l_ref, acc_ref, ot_ref,
                 *, lambda_init):
    blk = ATT_BLOCK
    j = pl.program_id(1)
    combos = [(DIFF_QK_DIM, 2 * h + i, h) for h in range(N_HEADS) for i in range(2)]
    lane = lax.broadcasted_iota(jnp.int32, (1, GW), 1)
    _att_init(q_ref[0], lane, combos, qm_ref, m_ref, l_ref, acc_ref)

    def body(n, carry):
        _att_keyblock(k_ref, vt_ref, n, combos, qm_ref, m_ref, l_ref, acc_ref, lambda c, key0, st: st)
        return carry

    lax.fori_loop(0, j, body, 0)
    _att_keyblock(k_ref, vt_ref, j, combos, qm_ref, m_ref, l_ref, acc_ref, _causal_mask)

    lqk = lqk_ref[...]
    lam = (jnp.exp(jnp.sum(lqk[0:1] * lqk[1:2], axis=1, keepdims=True))
           - jnp.exp(jnp.sum(lqk[2:3] * lqk[3:4], axis=1, keepdims=True)) + lambda_init)
    for h in range(N_HEADS):
        x = _att_result(2 * h, l_ref, acc_ref) - lam * _att_result(2 * h + 1, l_ref, acc_ref)
        ms = jnp.mean(x * x, axis=0, keepdims=True)
        ot_ref[h * HEAD_DIM:(h + 1) * HEAD_DIM, :] = x * lax.rsqrt(ms + DIFF_SUBLN_EPS)
    o_ref[0] = (ot_ref[...].T * g_ref[...] * (1.0 - lambda_init)).astype(BF16)


def _diff(q, k, vt, lqk, g_tiled, lambda_init):
    b, s, _ = q.shape
    blk = ATT_BLOCK
    nb = s // blk
    in_specs, out_spec, scratch = _att_specs(b, s, 2 * N_HEADS)
    return pl.pallas_call(
        functools.partial(_diff_kernel, lambda_init=lambda_init),
        grid=(b, nb),
        in_specs=in_specs + [pl.BlockSpec((4, 128), lambda i, j: (0, 0)),
                             pl.BlockSpec((1, GW), lambda i, j: (0, 0))],
        out_specs=out_spec,
        out_shape=jax.ShapeDtypeStruct((b, s, GW), BF16),
        scratch_shapes=scratch + [pltpu.VMEM((GW, blk), F32)],
        compiler_params=_params("parallel", "arbitrary"),
        name="diffattn",
    )(q, k, vt.reshape(b, nb, GW, blk), lqk, g_tiled)


def _sgu_kernel(pc_ref, lng_ref, lnb_ref, w_ref, bias_ref, o_ref):
    rows = pc_ref.shape[0]
    pc = pc_ref[...].astype(F32)
    u = _gelu(pc[:, :GW])
    v = _gelu(pc[:, GW:])
    mu = jnp.mean(v, axis=-1, keepdims=True)
    d = v - mu
    var = jnp.mean(d * d, axis=-1, keepdims=True)
    vn = d * lax.rsqrt(var + 1e-5) * lng_ref[...] + lnb_ref[...]
    lane = lax.broadcasted_iota(jnp.int32, (1, GW), 1)
    tr = lax.broadcasted_iota(jnp.int32, (SGU_CHUNK, SGU_CHUNK), 0)
    tc = lax.broadcasted_iota(jnp.int32, (SGU_CHUNK, SGU_CHUNK), 1)
    wm = [jnp.where(tr >= tc, w_ref[h], 0.0).astype(BF16) for h in range(N_HEADS)]
    for c in range(rows // SGU_CHUNK):
        sl = slice(c * SGU_CHUNK, (c + 1) * SGU_CHUNK)
        vc = vn[sl]
        mixed = bias_ref[...]
        for h in range(N_HEADS):
            vm = jnp.where((lane // HEAD_DIM) == h, vc, 0.0).astype(BF16)
            mixed = mixed + _dot(wm[h], vm)
        o_ref[sl, :] = (u[sl] * mixed).astype(BF16)


def _sgu(pc, lng, lnb, w, bias2d):
    t = pc.shape[0]
    tm = ROW_TILE
    const2 = lambda i: (0, 0)
    return pl.pallas_call(
        _sgu_kernel,
        grid=(t // tm,),
        in_specs=[pl.BlockSpec((tm, 2 * GW), lambda i: (i, 0)),
                  pl.BlockSpec((1, GW), const2), pl.BlockSpec((1, GW), const2),
                  pl.BlockSpec((N_HEADS, SGU_CHUNK, SGU_CHUNK), lambda i: (0, 0, 0)),
                  pl.BlockSpec((SGU_CHUNK, GW), const2)],
        out_specs=pl.BlockSpec((tm, GW), lambda i: (i, 0)),
        out_shape=jax.ShapeDtypeStruct((t, GW), BF16),
        compiler_params=_params("parallel"),
        name="sgu",
    )(pc, lng, lnb, w, bias2d)


def _rwkv_kernel(pb_ref, mu_ref, wl_ref, w0_ref, a0_ref, kk_ref, ka_ref, rk_ref, lng_ref, lnb_ref,
                 bd_ref, ltri_ref, o_ref,
                 state_ref, prev_ref, r_s, lw_s, k_s, v_s, al_s, be_s, y_s):
    rt = pb_ref.shape[1]
    c_len = RWKV_CHUNK

    @pl.when(pl.program_id(1) == 0)
    def _():
        state_ref[...] = jnp.zeros_like(state_ref)
        prev_ref[...] = jnp.zeros_like(prev_ref)

    bd = bd_ref[...]
    xt = pb_ref[0].astype(F32)
    rowid = lax.broadcasted_iota(jnp.int32, (rt, 1), 0)
    prev = jnp.where(rowid == 0, prev_ref[...], pltpu.roll(xt, 1, axis=0))
    prev_ref[...] = xt[rt - 1:rt, :]
    xs = xt + (prev - xt) * mu_ref[...]
    r = xs[:, 0:GW]
    k = xs[:, GW:2 * GW]
    v = xs[:, 2 * GW:3 * GW]
    lo = xs[:, 3 * GW:4 * GW]
    lane = lax.broadcasted_iota(jnp.int32, (1, GW), 1)
    lin = jnp.where(lane < 64, jnp.tanh(lo), jnp.where(lane < 128, lo, _sigmoid(lo)))
    z = _dot(lin.astype(BF16), wl_ref[...])
    wz = -(w0_ref[...] + z[:, 0:GW])
    softplus = jnp.maximum(wz, 0.0) + jnp.log(1.0 + jnp.exp(-jnp.abs(wz)))
    lw = -jnp.exp(-softplus - 0.5)
    a = _sigmoid(a0_ref[...] + z[:, GW:2 * GW])
    gate = z[:, 2 * GW:3 * GW]
    kk = k * kk_ref[...]
    kk = kk / jnp.maximum(jnp.sqrt(_segsum(kk * kk, bd)), 1e-12)
    k2 = k * (1.0 + (a - 1.0) * ka_ref[...])
    bonus = _segsum(r * k2 * rk_ref[...], bd) * v

    r_s[...] = r
    lw_s[...] = lw
    k_s[...] = k2
    v_s[...] = v
    al_s[...] = -kk
    be_s[...] = kk * a

    hms = [(lane // HEAD_DIM) == h for h in range(N_HEADS)]
    brow = lax.broadcasted_iota(jnp.int32, (GW, GW), 0)
    bcol = lax.broadcasted_iota(jnp.int32, (GW, GW), 1)
    strict = brow > bcol
    lower = brow >= bcol
    eye = brow == bcol
    ltri = ltri_ref[...]

    def big(x):
        return jnp.concatenate([jnp.where(hm, x, 0.0) for hm in hms], axis=0)

    def unbig(x):
        return x[0:c_len] + x[c_len:2 * c_len] + x[2 * c_len:3 * c_len] + x[3 * c_len:4 * c_len]

    def chunk_terms(sl, out):
        lwc = lw_s[sl, :]
        h3 = _split3(lwc)
        cs = _dot(ltri, h3[0]) + _dot(ltri, h3[1]) + _dot(ltri, h3[2])
        cs_end = cs[c_len - 1:c_len, :]
        e_neg = jnp.exp(-cs)
        e_rem = jnp.exp(cs_end - cs)
        al, be, kc, rc, vc = al_s[sl, :], be_s[sl, :], k_s[sl, :], r_s[sl, :], v_s[sl, :]
        r_t = rc * jnp.exp(cs)
        a_b = big(al * jnp.exp(cs - lwc)).astype(BF16)
        r_b = big(r_t).astype(BF16)
        bi_b = big(be * e_neg).astype(BF16)
        ki_b = big(kc * e_neg).astype(BF16)
        v_b = big(vc).astype(BF16)
        g = _dot_nt(jnp.concatenate([a_b, r_b], axis=0), jnp.concatenate([bi_b, ki_b], axis=0))
        yield
        a_ab = jnp.where(strict, g[:GW, :GW], 0.0)
        a_ak = jnp.where(strict, g[:GW, GW:], 0.0).astype(BF16)
        a_rb = jnp.where(lower, g[GW:, :GW], 0.0).astype(BF16)
        a_rk = jnp.where(lower, g[GW:, GW:], 0.0).astype(BF16)
        xb = a_ab.astype(BF16)
        tinv = jnp.where(eye, 1.0, a_ab)
        pb = _dot(xb, xb).astype(BF16)
        yield
        for _ in range(4):
            both = _dot(jnp.concatenate([pb, tinv.astype(BF16)], axis=0), pb)
            yield
            tinv = tinv + both[GW:]
            pb = both[:GW].astype(BF16)
        tb = (tinv + _dot(tinv.astype(BF16), pb)).astype(BF16)
        yield
        bd_t = big(be * e_rem).T.astype(BF16)
        kd_t = big(kc * e_rem).T.astype(BF16)
        on_v = _dot(jnp.concatenate([a_ak, a_rk, kd_t], axis=0), v_b)
        yield
        p1 = _dot(tb, a_b).astype(BF16)
        p2 = _dot(tb, on_v[:GW].astype(BF16)).astype(BF16)
        yield
        on_p = _dot(jnp.concatenate([a_rb, bd_t], axis=0), jnp.concatenate([p1, p2], axis=1))
        out.append((r_t + unbig(on_p[:GW, :GW]),
                    unbig(on_p[:GW, GW:] + on_v[GW:2 * GW]),
                    jnp.where(eye, jnp.exp(cs_end), 0.0) + on_p[GW:, :GW],
                    on_p[GW:, GW:] + on_v[2 * GW:]))
        yield

    def chunk_group(i, carry):
        slices = [pl.ds(pl.multiple_of((RWKV_INTERLEAVE * i + k) * c_len, c_len), c_len)
                  for k in range(RWKV_INTERLEAVE)]
        outs = [[] for _ in slices]
        stages = [chunk_terms(sl, out) for sl, out in zip(slices, outs)]
        for _ in zip(*stages):
            pass
        for sl, out in zip(slices, outs):
            q_tok, z_tok, m_c, n_c = out[0]
            skb = state_ref[...].astype(BF16)
            y_s[sl, :] = _dot(q_tok.astype(BF16), skb) + z_tok
            state_ref[...] = _dot(m_c.astype(BF16), skb) + n_c
        return carry

    lax.fori_loop(0, rt // (RWKV_INTERLEAVE * c_len), chunk_group, 0)

    y = y_s[...]
    mean = _segsum(y, bd) * (1.0 / HEAD_DIM)
    d = y - mean
    var = _segsum(d * d, bd) * (1.0 / HEAD_DIM)
    yn = d * lax.rsqrt(var + RWKV_LN_EPS) * lng_ref[...] + lnb_ref[...]
    o_ref[0] = ((yn + bonus) * gate).astype(BF16)


def _rwkv(pb, mu, wl, w0, a0, kk, ka, rk, lng, lnb, bd, ltri):
    b, s, _ = pb.shape
    rt = RWKV_ROWS
    c2 = lambda i, j: (0, 0)
    vec = pl.BlockSpec((1, GW), c2)
    return pl.pallas_call(
        _rwkv_kernel,
        grid=(b, s // rt),
        in_specs=[pl.BlockSpec((1, rt, 4 * GW), lambda i, j: (i, j, 0)),
                  pl.BlockSpec((1, 4 * GW), c2),
                  pl.BlockSpec((GW, 3 * GW), c2),
                  vec, vec, vec, vec, vec, vec, vec,
                  pl.BlockSpec((GW, GW), c2),
                  pl.BlockSpec((RWKV_CHUNK, RWKV_CHUNK), c2)],
        out_specs=pl.BlockSpec((1, rt, GW), lambda i, j: (i, j, 0)),
        out_shape=jax.ShapeDtypeStruct((b, s, GW), BF16),
        scratch_shapes=[pltpu.VMEM((GW, GW), F32), pltpu.VMEM((1, 4 * GW), F32)]
                       + [pltpu.VMEM((rt, GW), F32) for _ in range(7)],
        compiler_params=_params("parallel", "arbitrary"),
        name="rwkv7",
    )(pb, mu, wl, w0, a0, kk, ka, rk, lng, lnb, bd, ltri)


def _outproj_kernel(oa_ref, ob_ref, oc_ref, od_ref, x_ref, w_ref, g_ref, wrh_ref, wrl_ref, br_ref,
                    h_o, hn_o, eid_o, wts_o):
    mixed = jnp.concatenate([oa_ref[...], ob_ref[...], oc_ref[...], od_ref[...]], axis=1)
    h = x_ref[...] + _dot(mixed, w_ref[...])
    h_o[...] = h
    ms = jnp.mean(h * h, axis=-1, keepdims=True)
    hn = h * lax.rsqrt(ms + RMS_EPS) * g_ref[...]
    _store_row_tiles(hn_o, hn)
    hi = hn.astype(BF16)
    lo = (hn - hi.astype(F32)).astype(BF16)
    wrh = wrh_ref[...]
    logits = _dot(hi, wrh) + _dot(lo, wrh) + _dot(hi, wrl_ref[...]) + br_ref[...]
    tm = logits.shape[0]
    lane = lax.broadcasted_iota(jnp.int32, (tm, 128), 1)
    lanef = lane.astype(F32)
    big_idx = 1e9

    def rmax(t):
        return jnp.max(t, axis=1, keepdims=True)

    def rmin(t):
        return jnp.min(t, axis=1, keepdims=True)

    def rsum(t):
        return jnp.sum(t, axis=1, keepdims=True)

    gmask = lane < N_EXPERT_GROUPS
    gl = jnp.where(gmask, logits, NEG_INF)
    gmax = rmax(gl)
    gidx = rmin(jnp.where(gl == gmax, lanef, big_idx))
    gp = 1.0 / rsum(jnp.where(gmask, jnp.exp(logits - gmax), 0.0))
    e_lo = N_EXPERT_GROUPS + gidx * EXPERTS_PER_GROUP
    emask = (lanef >= e_lo) & (lanef < e_lo + EXPERTS_PER_GROUP)
    el = jnp.where(emask, logits, NEG_INF)
    emax = rmax(el)
    esum = rsum(jnp.where(emask, jnp.exp(logits - emax), 0.0))
    i1 = rmin(jnp.where(el == emax, lanef, big_idx))
    el2 = jnp.where(lanef == i1, NEG_INF, el)
    m2 = rmax(el2)
    i2 = rmin(jnp.where(el2 == m2, lanef, big_idx))
    p1 = 1.0 / esum
    p2 = jnp.exp(m2 - emax) / esum
    w1 = gp * (p1 / (p1 + p2))
    w2 = gp * (p2 / (p1 + p2))
    eid = jnp.where(lane == 0, i1 - N_EXPERT_GROUPS, jnp.where(lane == 1, i2 - N_EXPERT_GROUPS, 0.0))
    eid_o[...] = eid.astype(jnp.int32)
    wts_o[...] = jnp.where(lane == 0, w1, jnp.where(lane == 1, w2, 0.0))


def _outproj(oa, ob, oc, od, x2, w_out, g2, wr_hi, wr_lo, br):
    t = x2.shape[0]
    tm = ROW_TILE
    row = lambda i: (i, 0)
    const = lambda i: (0, 0)
    mix = pl.BlockSpec((tm, GW), row)
    return pl.pallas_call(
        _outproj_kernel,
        grid=(t // tm,),
        in_specs=[mix, mix, mix, mix,
                  pl.BlockSpec((tm, D_MODEL), row),
                  pl.BlockSpec((D_MODEL, D_MODEL), const),
                  pl.BlockSpec((1, D_MODEL), const),
                  pl.BlockSpec((D_MODEL, 128), const), pl.BlockSpec((D_MODEL, 128), const),
                  pl.BlockSpec((1, 128), const)],
        out_specs=[pl.BlockSpec((tm, D_MODEL), row), pl.BlockSpec((tm * ROW_TILE_ROWS, 128), row),
                   pl.BlockSpec((tm, 128), row), pl.BlockSpec((tm, 128), row)],
        out_shape=[jax.ShapeDtypeStruct((t, D_MODEL), F32), jax.ShapeDtypeStruct((t * ROW_TILE_ROWS, 128), F32),
                   jax.ShapeDtypeStruct((t, 128), jnp.int32), jax.ShapeDtypeStruct((t, 128), F32)],
        compiler_params=_params("parallel"),
        name="outproj_router",
    )(oa, ob, oc, od, x2, w_out, g2, wr_hi, wr_lo, br)


def _expert_kernel(blk_e_ref, blk_nv_ref, tok_ref, tokn_ref, dst_ref, hn_hbm, wg_ref, wu_ref, wd_ref, y_hbm,
                   xbuf, ybuf, wgb, wub, wdb, gsem, ssem):
    i = pl.program_id(0)
    nblk = pl.num_programs(0)
    slot = i % 2
    other = 1 - slot
    tr = ROW_TILE_ROWS
    full = xbuf.shape[1] // tr
    nv = blk_nv_ref[i]
    nv_next = jnp.where(i + 1 < nblk, blk_nv_ref[jnp.minimum(i + 1, nblk - 1)], 0)
    nv_prev = jnp.where(i >= 1, blk_nv_ref[jnp.maximum(i - 1, 0)], 0)
    nv_prev2 = jnp.where(i >= 2, blk_nv_ref[jnp.maximum(i - 2, 0)], 0)

    def tile_rows(start):
        return pl.ds(start if isinstance(start, int) else pl.multiple_of(start, tr), tr)

    def row_in(idx_ref, s, r):
        return pltpu.make_async_copy(hn_hbm.at[tile_rows(idx_ref[0, 0, r])], xbuf.at[s, tile_rows(r * tr)], gsem.at[s])

    def row_out(s, r):
        return pltpu.make_async_copy(ybuf.at[s, tile_rows(r * tr)], y_hbm.at[tile_rows(dst_ref[0, 0, r])], ssem.at[s])

    def for_rows(n, fn):
        def body(r, carry):
            fn(r)
            return carry
        lax.fori_loop(0, n, body, 0)

    def ffn():
        xb = _load_row_tiles(xbuf.at[slot], full).astype(BF16)
        hg = _dot(xb, wgb[...])
        hu = _dot(xb, wub[...])
        hdn = (hg * _sigmoid(hg) * hu).astype(BF16)
        _store_row_tiles(ybuf.at[slot], _dot(hdn, wdb[...]))

    def wait_rows(n, block_copy, row_copy):
        @pl.when(n == full)
        def _():
            block_copy.wait()

        @pl.when(n < full)
        def _():
            lax.fori_loop(0, n, lambda r, carry: (row_copy.wait(), carry)[1], 0)

    def wait_gather(n, s):
        wait_rows(n, pltpu.make_async_copy(hn_hbm.at[pl.ds(0, full * tr)], xbuf.at[s], gsem.at[s]),
                  pltpu.make_async_copy(hn_hbm.at[pl.ds(0, tr)], xbuf.at[s, pl.ds(0, tr)], gsem.at[s]))

    def wait_scatter(n, s):
        wait_rows(n, pltpu.make_async_copy(ybuf.at[s], y_hbm.at[pl.ds(0, full * tr)], ssem.at[s]),
                  pltpu.make_async_copy(ybuf.at[s, pl.ds(0, tr)], y_hbm.at[pl.ds(0, tr)], ssem.at[s]))

    @pl.when(i == 0)
    def _():
        xbuf[...] = jnp.zeros_like(xbuf)
        for_rows(nv, lambda r: row_in(tok_ref, slot, r).start())

    wait_gather(nv, slot)
    wait_scatter(nv_prev2, slot)

    @pl.when((i == 0) | (blk_e_ref[i] != blk_e_ref[jnp.maximum(i - 1, 0)]))
    def _():
        wgb[...] = wg_ref[0, 0].astype(BF16)
        wub[...] = wu_ref[0, 0].astype(BF16)
        wdb[...] = wd_ref[0, 0].astype(BF16)

    fast = (nv == full) & (nv_next == full)

    @pl.when(fast)
    def _():
        for r in range(full):
            row_in(tokn_ref, other, r).start(priority=r % 2)
        ffn()
        for r in range(full):
            row_out(slot, r).start(priority=r % 2)

    @pl.when(jnp.logical_not(fast))
    def _():
        for_rows(nv_next, lambda r: row_in(tokn_ref, other, r).start())

        @pl.when(nv > 0)
        def _():
            ffn()
            for_rows(nv, lambda r: row_out(slot, r).start())

    @pl.when(i == nblk - 1)
    def _():
        wait_scatter(nv_prev, other)
        wait_scatter(nv, slot)


def _experts(blk_e, blk_nv, row_tok, row_dst, hn, wg, wu, wd, layer, m_rows):
    nblk = blk_e.shape[0]
    r = MOE_ROWS
    grid_spec = pltpu.PrefetchScalarGridSpec(
        num_scalar_prefetch=2,
        grid=(nblk,),
        in_specs=[pl.BlockSpec((1, 1, r), lambda i, be, nv: (i, 0, 0), memory_space=pltpu.SMEM),
                  pl.BlockSpec((1, 1, r), lambda i, be, nv: (jnp.minimum(i + 1, nblk - 1), 0, 0),
                               memory_space=pltpu.SMEM),
                  pl.BlockSpec((1, 1, r), lambda i, be, nv: (i, 0, 0), memory_space=pltpu.SMEM),
                  pl.BlockSpec(memory_space=pl.ANY),
                  pl.BlockSpec((1, 1, D_MODEL, EXPERT_FF), lambda i, be, nv: (layer, be[i], 0, 0)),
                  pl.BlockSpec((1, 1, D_MODEL, EXPERT_FF), lambda i, be, nv: (layer, be[i], 0, 0)),
                  pl.BlockSpec((1, 1, EXPERT_FF, D_MODEL), lambda i, be, nv: (layer, be[i], 0, 0))],
        out_specs=pl.BlockSpec(memory_space=pl.ANY),
        scratch_shapes=[pltpu.VMEM((2, r * ROW_TILE_ROWS, 128), F32), pltpu.VMEM((2, r * ROW_TILE_ROWS, 128), F32),
                        pltpu.VMEM((D_MODEL, EXPERT_FF), BF16), pltpu.VMEM((D_MODEL, EXPERT_FF), BF16),
                        pltpu.VMEM((EXPERT_FF, D_MODEL), BF16),
                        pltpu.SemaphoreType.DMA((2,)), pltpu.SemaphoreType.DMA((2,))],
    )
    return pl.pallas_call(
        _expert_kernel,
        grid_spec=grid_spec,
        out_shape=jax.ShapeDtypeStruct((m_rows * ROW_TILE_ROWS, 128), F32),
        compiler_params=_params("arbitrary"),
        name="experts",
    )(blk_e, blk_nv, row_tok, row_tok, row_dst, hn, wg, wu, wd)


def _dispatch(eid, t):
    r = MOE_ROWS
    m = 2 * t
    nblk = m // r + N_EXPERTS
    experts = jnp.arange(N_EXPERTS, dtype=jnp.int32)
    e_flat = eid[:, :2].reshape(m)
    counts = jnp.sum((e_flat[:, None] == experts[None, :]).astype(jnp.int32), axis=0)
    n_pad = (r - counts % r) % r
    pad_keys = jnp.where(jnp.arange(r, dtype=jnp.int32)[None, :] < n_pad[:, None], experts[:, None], N_EXPERTS)
    keys = jnp.concatenate([e_flat, pad_keys.reshape(N_EXPERTS * r)])
    payload = jnp.concatenate([jnp.arange(m, dtype=jnp.int32), jnp.full((N_EXPERTS * r,), -1, jnp.int32)])
    keys_s, a = lax.sort((keys, payload), num_keys=1, is_stable=True)
    valid = a >= 0
    row_tok = jnp.where(valid, a // 2, 0) * ROW_TILE_ROWS
    row_dst = jnp.where(valid, (a % 2) * t + a // 2, 0) * ROW_TILE_ROWS
    blk_e = jnp.minimum(keys_s.reshape(nblk, r)[:, 0], N_EXPERTS - 1)
    blk_nv = jnp.sum(valid.reshape(nblk, r).astype(jnp.int32), axis=1)
    return blk_e, blk_nv, row_tok.reshape(nblk, 1, r), row_dst.reshape(nblk, 1, r)


def _combine_kernel(h_ref, y0_ref, y1_ref, wts_ref, fg_ref, o_ref, *, final):
    w = wts_ref[...]
    rows = h_ref.shape[0]
    x = (h_ref[...] + w[:, 0:1] * _load_row_tiles(y0_ref.at[0], rows)
         + w[:, 1:2] * _load_row_tiles(y1_ref.at[0], rows))
    if final:
        ms = jnp.mean(x * x, axis=-1, keepdims=True)
        x = x * lax.rsqrt(ms + RMS_EPS) * fg_ref[...]
    o_ref[...] = x


def _combine(h, y, wts, fg, final):
    t = h.shape[0]
    tm = ROW_TILE
    row = lambda i: (i, 0)
    return pl.pallas_call(
        functools.partial(_combine_kernel, final=final),
        grid=(t // tm,),
        in_specs=[pl.BlockSpec((tm, D_MODEL), row),
                  pl.BlockSpec((1, tm * ROW_TILE_ROWS, 128), lambda i: (0, i, 0)),
                  pl.BlockSpec((1, tm * ROW_TILE_ROWS, 128), lambda i: (1, i, 0)),
                  pl.BlockSpec((tm, 128), row),
                  pl.BlockSpec((1, D_MODEL), lambda i: (0, 0))],
        out_specs=pl.BlockSpec((tm, D_MODEL), row),
        out_shape=jax.ShapeDtypeStruct((t, D_MODEL), F32),
        compiler_params=_params("parallel"),
        name="combine",
    )(h, y, y, wts, fg)


def _rope_tables(seq, head_width, rot_dim):
    half = rot_dim // 2
    inv = ROPE_THETA ** (-jnp.arange(0, rot_dim, 2, dtype=F32) / rot_dim)
    ang = jnp.arange(seq, dtype=F32)[:, None] * inv[None, :]
    cos, sin = jnp.cos(ang), jnp.sin(ang)
    d = np.arange(GW) % head_width
    idx = d % half
    cos_t = jnp.where((d < rot_dim)[None, :], cos[:, idx], 1.0)
    sin_t = jnp.where((d < half)[None, :], -sin[:, idx], jnp.where((d < rot_dim)[None, :], sin[:, idx], 0.0))
    return cos_t.astype(F32), sin_t.astype(F32)


def _swap_cols(w, head_width, rot_dim):
    half = rot_dim // 2
    c = np.arange(GW)
    d = c % head_width
    partner = np.where(d < half, c + half, np.where(d < rot_dim, c - half, c))
    keep = (d < rot_dim).astype(np.float32)
    return w[:, partner] * keep[None, :]


def _layer_weights(l, w_in, rwkv_mu, rwkv_w2, rwkv_a2, rwkv_g2, sgu_w, sgu_b, diff_l, diff_subln_g,
                   router_group_w, router_group_b, router_expert_w, router_expert_b):
    wi = w_in[l]
    a0, b0, c0, d0 = 0, 3 * GW, 3 * GW + 4 * GW, 3 * GW + 4 * GW + 2 * GW
    qa, ka, va = wi[:, a0:a0 + GW], wi[:, a0 + GW:a0 + 2 * GW], wi[:, a0 + 2 * GW:a0 + 3 * GW]
    bsl = [(0, 256), (320, 576), (576, 832), (256, 320), (832, 896), (896, 1024)]
    wb = jnp.concatenate([wi[:, b0 + s:b0 + e] for s, e in bsl], axis=1)
    mu = jnp.concatenate([rwkv_mu[l][s:e] for s, e in bsl])[None, :]
    wc = wi[:, c0:c0 + 2 * GW]
    qd, kd, vd = wi[:, d0:d0 + GW], wi[:, d0 + GW:d0 + 2 * GW], wi[:, d0 + 2 * GW:d0 + 3 * GW]
    w_all = jnp.concatenate(
        [qa, _swap_cols(qa, HEAD_DIM, 16), ka, _swap_cols(ka, HEAD_DIM, 16), wb, wc,
         qd, _swap_cols(qd, DIFF_QK_DIM, 8), kd, _swap_cols(kd, DIFF_QK_DIM, 8)], axis=1).astype(BF16)
    wvt = jnp.concatenate([va, vd], axis=1).T.astype(BF16)
    wl = jnp.zeros((GW, 3 * GW), F32)
    wl = wl.at[0:64, 0:GW].set(rwkv_w2[l]).at[64:128, GW:2 * GW].set(rwkv_a2[l]).at[128:256, 2 * GW:].set(rwkv_g2[l])
    bias2d = jnp.repeat(sgu_b[l].T, HEAD_DIM, axis=1)
    lqk = jnp.zeros((4, 128), F32).at[:, :DIFF_QK_DIM].set(jnp.stack([d[l] for d in diff_l]))
    wr = jnp.zeros((D_MODEL, 128), F32).at[:, :N_EXPERT_GROUPS].set(router_group_w[l])
    wr = wr.at[:, N_EXPERT_GROUPS:N_EXPERT_GROUPS + N_EXPERTS].set(router_expert_w[l])
    wr_hi = wr.astype(BF16)
    wr_lo = (wr - wr_hi.astype(F32)).astype(BF16)
    br = jnp.zeros((1, 128), F32).at[0, :N_EXPERT_GROUPS].set(router_group_b[l])
    br = br.at[0, N_EXPERT_GROUPS:N_EXPERT_GROUPS + N_EXPERTS].set(router_expert_b[l])
    return dict(w_all=w_all, wvt=wvt, mu=mu, wl=wl.astype(BF16), bias2d=bias2d, lqk=lqk,
                subln=jnp.tile(diff_subln_g[l], N_HEADS)[None, :], wr_hi=wr_hi, wr_lo=wr_lo, br=br)


def kernel(x, norm1_g, w_in, rwkv_mu, rwkv_w0, rwkv_w2, rwkv_a0, rwkv_a2, rwkv_g2, rwkv_kk, rwkv_ka, rwkv_rk, rwkv_ln_g, rwkv_ln_b, sgu_ln_g, sgu_ln_b, sgu_w, sgu_b, diff_lq1, diff_lk1, diff_lq2, diff_lk2, diff_subln_g, w_out, norm2_g, router_group_w, router_group_b, router_expert_w, router_expert_b, moe_w_gate, moe_w_up, moe_w_down, final_g):
    b, s, d = x.shape
    depth = w_in.shape[0]
    t = b * s
    ca, sa = _rope_tables(s, HEAD_DIM, 16)
    cd, sd = _rope_tables(s, DIFF_QK_DIM, 8)
    seg = np.arange(GW) // HEAD_DIM
    bd = jnp.asarray(seg[:, None] == seg[None, :], BF16)
    ltri = jnp.asarray(np.tril(np.ones((RWKV_CHUNK, RWKV_CHUNK))), BF16)
    row = lambda p: p[None, :]
    x2 = x.reshape(t, d)
    for l in range(depth):
        lw = _layer_weights(l, w_in, rwkv_mu, rwkv_w2, rwkv_a2, rwkv_g2, sgu_w, sgu_b,
                            (diff_lq1, diff_lk1, diff_lq2, diff_lk2), diff_subln_g,
                            router_group_w, router_group_b, router_expert_w, router_expert_b)
        qa, ka, va, pb, pc, qd, kd, vd = _inproj(x2, row(norm1_g[l]), lw["w_all"], lw["wvt"], ca, sa, cd, sd, s)
        seq3 = lambda z: z.reshape(b, s, z.shape[-1])
        out_a = _moba(seq3(qa), seq3(ka), va)
        out_b = _rwkv(seq3(pb), lw["mu"], lw["wl"], row(rwkv_w0[l]), row(rwkv_a0[l]), row(rwkv_kk[l]),
                      row(rwkv_ka[l]), row(rwkv_rk[l].reshape(GW)), row(rwkv_ln_g[l]), row(rwkv_ln_b[l]), bd, ltri)
        out_c = _sgu(pc, row(sgu_ln_g[l]), row(sgu_ln_b[l]), sgu_w[l], lw["bias2d"])
        lambda_init = 0.8 - 0.6 * math.exp(-0.3 * l)
        out_d = _diff(seq3(qd), seq3(kd), vd, lw["lqk"], lw["subln"], lambda_init)
        h, hn, eid, wts = _outproj(out_a.reshape(t, GW), out_b.reshape(t, GW), out_c, out_d.reshape(t, GW),
                                   x2, w_out[l].astype(BF16), row(norm2_g[l]), lw["wr_hi"], lw["wr_lo"], lw["br"])
        blk_e, blk_nv, row_tok, row_dst = _dispatch(eid, t)
        y = _experts(blk_e, blk_nv, row_tok, row_dst, hn, moe_w_gate, moe_w_up, moe_w_down, l, 2 * t)
        x2 = _combine(h, y.reshape(2, t * ROW_TILE_ROWS, 128), wts, row(final_g), l == depth - 1)
    return x2.reshape(b, s, d)
```

```python
import functools
import math

import numpy as np
import jax
import jax.numpy as jnp
from jax import lax
from jax.experimental import pallas as pl
from jax.experimental.pallas import tpu as pltpu

F32 = jnp.float32
BF16 = jnp.bfloat16

D_MODEL = 1024
HEAD_DIM = 64
N_HEADS = 4
GW = N_HEADS * HEAD_DIM
ROPE_THETA = 500000.0
RMS_EPS = 1e-6

ATT_BLOCK = 256
LOG2E = 1.4426950408889634
MOBA_TOPK = 3
RWKV_LN_EPS = 64e-5
RWKV_CHUNK = 64
RWKV_ROWS = 256
RWKV_INTERLEAVE = 4
SGU_CHUNK = 128
DIFF_QK_DIM = 32
DIFF_SUBLN_EPS = 1e-5

N_EXPERT_GROUPS = 4
EXPERTS_PER_GROUP = 8
N_EXPERTS = 32
EXPERT_FF = 512
MOE_ROWS = 256

ROW_TILE = 512
VMEM_LIMIT = 56 * 1024 * 1024

NEG_INF = float("-inf")


def _dot(a, b):
    return jnp.dot(a, b, preferred_element_type=F32)


def _dot_nt(a, b):
    return lax.dot_general(a, b, (((1,), (1,)), ((), ())), preferred_element_type=F32)


def _split3(x):
    hi = x.astype(BF16)
    r1 = x - hi.astype(F32)
    mid = r1.astype(BF16)
    lo = (r1 - mid.astype(F32)).astype(BF16)
    return hi, mid, lo


def _segsum(x, bd):
    hi, mid, lo = _split3(x)
    return _dot(hi, bd) + _dot(mid, bd) + _dot(lo, bd)


def _sigmoid(x):
    return 1.0 / (1.0 + jnp.exp(-x))


def _gelu(x):
    return 0.5 * x * (1.0 + lax.erf(x * (1.0 / math.sqrt(2.0))))


ROW_TILE_ROWS = D_MODEL // 128


def _store_row_tiles(ref, x):
    rows = x.shape[0]
    for k in range(ROW_TILE_ROWS):
        ref[pl.ds(k, rows, stride=ROW_TILE_ROWS), :] = x[:, k * 128:(k + 1) * 128]


def _load_row_tiles(ref, rows):
    return jnp.concatenate([ref[pl.ds(k, rows, stride=ROW_TILE_ROWS), :] for k in range(ROW_TILE_ROWS)], axis=1)


def _params(*sem):
    return pltpu.CompilerParams(dimension_semantics=sem, vmem_limit_bytes=VMEM_LIMIT)


def _inproj_kernel(x_ref, g_ref, w_ref, wvt_ref, ca_ref, sa_ref, cd_ref, sd_ref,
                   qa_o, ka_o, vta_o, pb_o, pc_o, qd_o, kd_o, vtd_o):
    x = x_ref[...]
    ms = jnp.mean(x * x, axis=-1, keepdims=True)
    xn = (x * lax.rsqrt(ms + RMS_EPS) * g_ref[...]).astype(BF16)

    def proj(c0, n=1):
        return _dot(xn, w_ref[:, c0 * GW:(c0 + n) * GW])

    ca, sa = ca_ref[...], sa_ref[...]
    qa_o[...] = ((proj(0) * ca + proj(1) * sa) * (HEAD_DIM ** -0.5 * LOG2E)).astype(BF16)
    ka_o[...] = (proj(2) * ca + proj(3) * sa).astype(BF16)
    pb_o[...] = proj(4, 4).astype(BF16)
    pc_o[...] = proj(8, 2).astype(BF16)
    cd, sd = cd_ref[...], sd_ref[...]
    qd_o[...] = ((proj(10) * cd + proj(11) * sd) * (DIFF_QK_DIM ** -0.5 * LOG2E)).astype(BF16)
    kd_o[...] = (proj(12) * cd + proj(13) * sd).astype(BF16)
    vt = _dot_nt(wvt_ref[...], xn)
    for kb in range(x.shape[0] // ATT_BLOCK):
        cols = slice(kb * ATT_BLOCK, (kb + 1) * ATT_BLOCK)
        vta_o[kb] = vt[0:GW, cols].astype(BF16)
        vtd_o[kb] = vt[GW:2 * GW, cols].astype(BF16)


def _inproj(x2, g, w_all, wvt, ca, sa, cd, sd, seq):
    t = x2.shape[0]
    tm = min(ROW_TILE, seq)
    per_seq = seq // tm
    kbs = tm // ATT_BLOCK
    row = lambda i: (i, 0)
    tab = lambda i: (i % per_seq, 0)
    const = lambda i: (0, 0)
    vt_spec = pl.BlockSpec((kbs, GW, ATT_BLOCK), lambda i: (i, 0, 0))
    vt_shape = jax.ShapeDtypeStruct((t // ATT_BLOCK, GW, ATT_BLOCK), BF16)
    rows = lambda w: (pl.BlockSpec((tm, w), row), jax.ShapeDtypeStruct((t, w), BF16))
    outs = [rows(GW), rows(GW), (vt_spec, vt_shape), rows(4 * GW), rows(2 * GW), rows(GW), rows(GW),
            (vt_spec, vt_shape)]
    return pl.pallas_call(
        _inproj_kernel,
        grid=(t // tm,),
        in_specs=[pl.BlockSpec((tm, D_MODEL), row),
                  pl.BlockSpec((1, D_MODEL), const),
                  pl.BlockSpec((D_MODEL, 14 * GW), const),
                  pl.BlockSpec((2 * GW, D_MODEL), const),
                  pl.BlockSpec((tm, GW), tab), pl.BlockSpec((tm, GW), tab),
                  pl.BlockSpec((tm, GW), tab), pl.BlockSpec((tm, GW), tab)],
        out_specs=[o[0] for o in outs],
        out_shape=[o[1] for o in outs],
        compiler_params=_params("parallel"),
        name="inproj",
    )(x2, g, w_all, wvt, ca, sa, cd, sd)


M_INIT = -1e30


def _att_init(q, combos, qm_ref, m_ref, l_ref, acc_ref):
    lane = lax.broadcasted_iota(jnp.int32, (1, GW), 1)
    for c, (width, idx, _) in enumerate(combos):
        qm_ref[c] = jnp.where((lane // width) == idx, q, jnp.zeros_like(q))
    m_ref[...] = jnp.full_like(m_ref, M_INIT)
    l_ref[...] = jnp.zeros_like(l_ref)
    acc_ref[...] = jnp.zeros_like(acc_ref)


def _att_update(c, st, vt_sub, m_ref, l_ref, acc_ref):
    ms = slice(c * 8, c * 8 + 1)
    rows = slice(c * HEAD_DIM, (c + 1) * HEAD_DIM)
    m_old = m_ref[ms, :]
    m_new = jnp.maximum(m_old, jnp.max(st, axis=0, keepdims=True))
    alpha = jnp.exp2(m_old - m_new)
    pt = jnp.exp2(st - m_new)
    l_ref[ms, :] = alpha * l_ref[ms, :] + jnp.sum(pt, axis=0, keepdims=True)
    acc_ref[rows, :] = alpha * acc_ref[rows, :] + _dot(vt_sub, pt.astype(BF16))
    m_ref[ms, :] = m_new


def _att_keyblock(k_ref, vt_ref, n, combos, qm_ref, m_ref, l_ref, acc_ref, mask_fn):
    kb = k_ref[0, pl.ds(pl.multiple_of(n * ATT_BLOCK, ATT_BLOCK), ATT_BLOCK), :]
    st_next = _dot_nt(kb, qm_ref[0])
    for c, (_, _, head) in enumerate(combos):
        st = st_next
        if c + 1 < len(combos):
            st_next = _dot_nt(kb, qm_ref[c + 1])
        vt_sub = vt_ref[0, n, head * HEAD_DIM:(head + 1) * HEAD_DIM, :]
        _att_update(c, mask_fn(c, st), vt_sub, m_ref, l_ref, acc_ref)


def _causal_mask(c, st):
    krow = lax.broadcasted_iota(jnp.int32, st.shape, 0)
    qcol = lax.broadcasted_iota(jnp.int32, st.shape, 1)
    return jnp.where(krow <= qcol, st, NEG_INF)


def _att_result(c, l_ref, acc_ref):
    return acc_ref[c * HEAD_DIM:(c + 1) * HEAD_DIM, :] / l_ref[c * 8:c * 8 + 1, :]


def _moba_kernel(q_ref, k_ref, vt_ref, avg_ref, o_ref, qm_ref, m_ref, l_ref, acc_ref, sel_ref, ot_ref, km_ref,
                 *, nb, n_sel):
    blk = ATT_BLOCK
    j = pl.program_id(1)
    combos = [(HEAD_DIM, h, h) for h in range(N_HEADS)]
    lane = lax.broadcasted_iota(jnp.int32, (1, GW), 1)
    _att_init(q_ref[0], combos, qm_ref, m_ref, l_ref, acc_ref)

    @pl.when(j == 0)
    def _():
        kmean = _dot(avg_ref[...], k_ref[0])
        for c in range(N_HEADS):
            kmc = jnp.where((lane // HEAD_DIM) == c, kmean, 0.0)
            hi = kmc.astype(BF16)
            km_ref[c * 16:c * 16 + 8, :] = hi.astype(F32)
            km_ref[c * 16 + 8:(c + 1) * 16, :] = kmc - hi.astype(F32)

    gates = _dot_nt(km_ref[...].astype(BF16), q_ref[0])
    nrow = lax.broadcasted_iota(jnp.int32, (8, blk), 0)
    past = nrow < j
    for c in range(N_HEADS):
        gate = jnp.where(past, gates[c * 16:c * 16 + 8] + gates[c * 16 + 8:(c + 1) * 16], NEG_INF)
        rank = jnp.zeros((8, blk), F32)
        for mblk in range(nb):
            gm = gate[mblk:mblk + 1, :]
            rank = rank + jnp.where(gm > gate, 1.0, jnp.where((gm == gate) & (nrow > mblk), 1.0, 0.0))
        sel_ref[c * 8:(c + 1) * 8, :] = jnp.where(past & (rank < n_sel), 1.0, 0.0)

    def body(n, carry):
        def selected(c, st):
            return jnp.where(sel_ref[pl.ds(c * 8 + n, 1), :] > 0.5, st, NEG_INF)
        _att_keyblock(k_ref, vt_ref, n, combos, qm_ref, m_ref, l_ref, acc_ref, selected)
        return carry

    lax.fori_loop(0, j, body, 0)
    _att_keyblock(k_ref, vt_ref, j, combos, qm_ref, m_ref, l_ref, acc_ref, _causal_mask)
    for c in range(N_HEADS):
        ot_ref[c * HEAD_DIM:(c + 1) * HEAD_DIM, :] = _att_result(c, l_ref, acc_ref)
    o_ref[0] = ot_ref[...].T.astype(BF16)


def _att_specs(b, s, n_combos):
    blk = ATT_BLOCK
    nb = s // blk
    in_specs = [pl.BlockSpec((1, blk, GW), lambda i, j: (i, j, 0)),
                pl.BlockSpec((1, s, GW), lambda i, j: (i, 0, 0)),
                pl.BlockSpec((1, nb, GW, blk), lambda i, j: (i, 0, 0, 0))]
    out_spec = pl.BlockSpec((1, blk, GW), lambda i, j: (i, j, 0))
    scratch = [pltpu.VMEM((n_combos, blk, GW), BF16),
               pltpu.VMEM((n_combos * 8, blk), F32), pltpu.VMEM((n_combos * 8, blk), F32),
               pltpu.VMEM((n_combos * HEAD_DIM, blk), F32)]
    return in_specs, out_spec, scratch


def _moba(q, k, vt):
    b, s, _ = q.shape
    blk = ATT_BLOCK
    nb = s // blk
    assert nb <= 8
    n_sel = min(MOBA_TOPK, max(nb - 1, 1))
    avg = jnp.where(jnp.arange(s)[None, :] // blk == jnp.arange(8)[:, None], 1.0 / blk, 0.0).astype(BF16)
    in_specs, out_spec, scratch = _att_specs(b, s, N_HEADS)
    return pl.pallas_call(
        functools.partial(_moba_kernel, nb=nb, n_sel=n_sel),
        grid=(b, nb),
        in_specs=in_specs + [pl.BlockSpec((8, s), lambda i, j: (0, 0))],
        out_specs=out_spec,
        out_shape=jax.ShapeDtypeStruct((b, s, GW), BF16),
        scratch_shapes=scratch + [pltpu.VMEM((N_HEADS * 8, blk), F32), pltpu.VMEM((GW, blk), F32),
                                  pltpu.VMEM((N_HEADS * 16, GW), F32)],
        compiler_params=_params("parallel", "arbitrary"),
        name="moba",
    )(q, k, vt.reshape(b, nb, GW, blk), avg)


def _diff_kernel(q_ref, k_ref, vt_ref, lqk_ref, g_ref, o_ref, qm_ref, m_ref, l_ref, acc_ref, ot_ref,
                 *, lambda_init):
    j = pl.program_id(1)
    combos = [(DIFF_QK_DIM, 2 * h + i, h) for h in range(N_HEADS) for i in range(2)]
    _att_init(q_ref[0], combos, qm_ref, m_ref, l_ref, acc_ref)

    def body(n, carry):
        _att_keyblock(k_ref, vt_ref, n, combos, qm_ref, m_ref, l_ref, acc_ref, lambda c, st: st)
        return carry

    lax.fori_loop(0, j, body, 0)
    _att_keyblock(k_ref, vt_ref, j, combos, qm_ref, m_ref, l_ref, acc_ref, _causal_mask)

    lqk = lqk_ref[...]
    lam = (jnp.exp(jnp.sum(lqk[0:1] * lqk[1:2], axis=1, keepdims=True))
           - jnp.exp(jnp.sum(lqk[2:3] * lqk[3:4], axis=1, keepdims=True)) + lambda_init)
    for h in range(N_HEADS):
        x = _att_result(2 * h, l_ref, acc_ref) - lam * _att_result(2 * h + 1, l_ref, acc_ref)
        ms = jnp.mean(x * x, axis=0, keepdims=True)
        ot_ref[h * HEAD_DIM:(h + 1) * HEAD_DIM, :] = x * lax.rsqrt(ms + DIFF_SUBLN_EPS)
    o_ref[0] = (ot_ref[...].T * g_ref[...] * (1.0 - lambda_init)).astype(BF16)


def _diff(q, k, vt, lqk, g_tiled, lambda_init):
    b, s, _ = q.shape
    blk = ATT_BLOCK
    nb = s // blk
    in_specs, out_spec, scratch = _att_specs(b, s, 2 * N_HEADS)
    return pl.pallas_call(
        functools.partial(_diff_kernel, lambda_init=lambda_init),
        grid=(b, nb),
        in_specs=in_specs + [pl.BlockSpec((4, 128), lambda i, j: (0, 0)),
                             pl.BlockSpec((1, GW), lambda i, j: (0, 0))],
        out_specs=out_spec,
        out_shape=jax.ShapeDtypeStruct((b, s, GW), BF16),
        scratch_shapes=scratch + [pltpu.VMEM((GW, blk), F32)],
        compiler_params=_params("parallel", "arbitrary"),
        name="diffattn",
    )(q, k, vt.reshape(b, nb, GW, blk), lqk, g_tiled)


def _sgu_kernel(pc_ref, lng_ref, lnb_ref, w_ref, bias_ref, o_ref):
    rows = pc_ref.shape[0]
    pc = pc_ref[...].astype(F32)
    u = _gelu(pc[:, :GW])
    v = _gelu(pc[:, GW:])
    mu = jnp.mean(v, axis=-1, keepdims=True)
    d = v - mu
    var = jnp.mean(d * d, axis=-1, keepdims=True)
    vn = d * lax.rsqrt(var + 1e-5) * lng_ref[...] + lnb_ref[...]
    lane = lax.broadcasted_iota(jnp.int32, (1, GW), 1)
    tr = lax.broadcasted_iota(jnp.int32, (SGU_CHUNK, SGU_CHUNK), 0)
    tc = lax.broadcasted_iota(jnp.int32, (SGU_CHUNK, SGU_CHUNK), 1)
    wm = [jnp.where(tr >= tc, w_ref[h], 0.0).astype(BF16) for h in range(N_HEADS)]
    for c in range(rows // SGU_CHUNK):
        sl = slice(c * SGU_CHUNK, (c + 1) * SGU_CHUNK)
        vc = vn[sl]
        mixed = bias_ref[...]
        for h in range(N_HEADS):
            vm = jnp.where((lane // HEAD_DIM) == h, vc, 0.0).astype(BF16)
            mixed = mixed + _dot(wm[h], vm)
        o_ref[sl, :] = (u[sl] * mixed).astype(BF16)


def _sgu(pc, lng, lnb, w, bias2d):
    t = pc.shape[0]
    tm = ROW_TILE
    const2 = lambda i: (0, 0)
    return pl.pallas_call(
        _sgu_kernel,
        grid=(t // tm,),
        in_specs=[pl.BlockSpec((tm, 2 * GW), lambda i: (i, 0)),
                  pl.BlockSpec((1, GW), const2), pl.BlockSpec((1, GW), const2),
                  pl.BlockSpec((N_HEADS, SGU_CHUNK, SGU_CHUNK), lambda i: (0, 0, 0)),
                  pl.BlockSpec((SGU_CHUNK, GW), const2)],
        out_specs=pl.BlockSpec((tm, GW), lambda i: (i, 0)),
        out_shape=jax.ShapeDtypeStruct((t, GW), BF16),
        compiler_params=_params("parallel"),
        name="sgu",
    )(pc, lng, lnb, w, bias2d)


def _rwkv_kernel(pb_ref, mu_ref, wl_ref, w0_ref, a0_ref, kk_ref, ka_ref, rk_ref, lng_ref, lnb_ref,
                 bd_ref, ltri_ref, o_ref,
                 state_ref, prev_ref, r_s, lw_s, k_s, v_s, al_s, be_s, y_s):
    rt = pb_ref.shape[1]
    c_len = RWKV_CHUNK

    @pl.when(pl.program_id(1) == 0)
    def _():
        state_ref[...] = jnp.zeros_like(state_ref)
        prev_ref[...] = jnp.zeros_like(prev_ref)

    bd = bd_ref[...]
    xt = pb_ref[0].astype(F32)
    rowid = lax.broadcasted_iota(jnp.int32, (rt, 1), 0)
    prev = jnp.where(rowid == 0, prev_ref[...], pltpu.roll(xt, 1, axis=0))
    prev_ref[...] = xt[rt - 1:rt, :]
    xs = xt + (prev - xt) * mu_ref[...]
    r = xs[:, 0:GW]
    k = xs[:, GW:2 * GW]
    v = xs[:, 2 * GW:3 * GW]
    lo = xs[:, 3 * GW:4 * GW]
    lane = lax.broadcasted_iota(jnp.int32, (1, GW), 1)
    lin = jnp.where(lane < 64, jnp.tanh(lo), jnp.where(lane < 128, lo, _sigmoid(lo)))
    z = _dot(lin.astype(BF16), wl_ref[...])
    wz = -(w0_ref[...] + z[:, 0:GW])
    softplus = jnp.maximum(wz, 0.0) + jnp.log(1.0 + jnp.exp(-jnp.abs(wz)))
    lw = -jnp.exp(-softplus - 0.5)
    a = _sigmoid(a0_ref[...] + z[:, GW:2 * GW])
    gate = z[:, 2 * GW:3 * GW]
    kk = k * kk_ref[...]
    kk = kk / jnp.maximum(jnp.sqrt(_segsum(kk * kk, bd)), 1e-12)
    k2 = k * (1.0 + (a - 1.0) * ka_ref[...])
    bonus = _segsum(r * k2 * rk_ref[...], bd) * v

    r_s[...] = r
    lw_s[...] = lw
    k_s[...] = k2
    v_s[...] = v
    al_s[...] = -kk
    be_s[...] = kk * a

    hms = [(lane // HEAD_DIM) == h for h in range(N_HEADS)]
    brow = lax.broadcasted_iota(jnp.int32, (GW, GW), 0)
    bcol = lax.broadcasted_iota(jnp.int32, (GW, GW), 1)
    strict = brow > bcol
    lower = brow >= bcol
    eye = brow == bcol
    ltri = ltri_ref[...]

    def big(x):
        return jnp.concatenate([jnp.where(hm, x, 0.0) for hm in hms], axis=0)

    def unbig(x):
        return x[0:c_len] + x[c_len:2 * c_len] + x[2 * c_len:3 * c_len] + x[3 * c_len:4 * c_len]

    def chunk_terms(sl, out):
        lwc = lw_s[sl, :]
        h3 = _split3(lwc)
        cs = _dot(ltri, h3[0]) + _dot(ltri, h3[1]) + _dot(ltri, h3[2])
        cs_end = cs[c_len - 1:c_len, :]
        e_neg = jnp.exp(-cs)
        e_rem = jnp.exp(cs_end - cs)
        al, be, kc, rc, vc = al_s[sl, :], be_s[sl, :], k_s[sl, :], r_s[sl, :], v_s[sl, :]
        r_t = rc * jnp.exp(cs)
        a_b = big(al * jnp.exp(cs - lwc)).astype(BF16)
        r_b = big(r_t).astype(BF16)
        bi_b = big(be * e_neg).astype(BF16)
        ki_b = big(kc * e_neg).astype(BF16)
        v_b = big(vc).astype(BF16)
        g = _dot_nt(jnp.concatenate([a_b, r_b], axis=0), jnp.concatenate([bi_b, ki_b], axis=0))
        yield
        a_ab = jnp.where(strict, g[:GW, :GW], 0.0)
        a_ak = jnp.where(strict, g[:GW, GW:], 0.0).astype(BF16)
        a_rb = jnp.where(lower, g[GW:, :GW], 0.0).astype(BF16)
        a_rk = jnp.where(lower, g[GW:, GW:], 0.0).astype(BF16)
        xb = a_ab.astype(BF16)
        tinv = jnp.where(eye, 1.0, a_ab)
        pb = _dot(xb, xb).astype(BF16)
        yield
        for _ in range(4):
            both = _dot(jnp.concatenate([pb, tinv.astype(BF16)], axis=0), pb)
            yield
            tinv = tinv + both[GW:]
            pb = both[:GW].astype(BF16)
        tb = (tinv + _dot(tinv.astype(BF16), pb)).astype(BF16)
        yield
        bd_t = big(be * e_rem).T.astype(BF16)
        kd_t = big(kc * e_rem).T.astype(BF16)
        on_v = _dot(jnp.concatenate([a_ak, a_rk, kd_t], axis=0), v_b)
        yield
        p1 = _dot(tb, a_b).astype(BF16)
        p2 = _dot(tb, on_v[:GW].astype(BF16)).astype(BF16)
        yield
        on_p = _dot(jnp.concatenate([a_rb, bd_t], axis=0), jnp.concatenate([p1, p2], axis=1))
        out.append((r_t + unbig(on_p[:GW, :GW]),
                    unbig(on_p[:GW, GW:] + on_v[GW:2 * GW]),
                    jnp.where(eye, jnp.exp(cs_end), 0.0) + on_p[GW:, :GW],
                    on_p[GW:, GW:] + on_v[2 * GW:]))
        yield

    def chunk_group(i, carry):
        slices = [pl.ds(pl.multiple_of((RWKV_INTERLEAVE * i + k) * c_len, c_len), c_len)
                  for k in range(RWKV_INTERLEAVE)]
        outs = [[] for _ in slices]
        stages = [chunk_terms(sl, out) for sl, out in zip(slices, outs)]
        for _ in zip(*stages):
            pass
        for sl, out in zip(slices, outs):
            q_tok, z_tok, m_c, n_c = out[0]
            skb = state_ref[...].astype(BF16)
            y_s[sl, :] = _dot(q_tok.astype(BF16), skb) + z_tok
            state_ref[...] = _dot(m_c.astype(BF16), skb) + n_c
        return carry

    lax.fori_loop(0, rt // (RWKV_INTERLEAVE * c_len), chunk_group, 0)

    y = y_s[...]
    mean = _segsum(y, bd) * (1.0 / HEAD_DIM)
    d = y - mean
    var = _segsum(d * d, bd) * (1.0 / HEAD_DIM)
    yn = d * lax.rsqrt(var + RWKV_LN_EPS) * lng_ref[...] + lnb_ref[...]
    o_ref[0] = ((yn + bonus) * gate).astype(BF16)


def _rwkv(pb, mu, wl, w0, a0, kk, ka, rk, lng, lnb, bd, ltri):
    b, s, _ = pb.shape
    rt = RWKV_ROWS
    c2 = lambda i, j: (0, 0)
    vec = pl.BlockSpec((1, GW), c2)
    return pl.pallas_call(
        _rwkv_kernel,
        grid=(b, s // rt),
        in_specs=[pl.BlockSpec((1, rt, 4 * GW), lambda i, j: (i, j, 0)),
                  pl.BlockSpec((1, 4 * GW), c2),
                  pl.BlockSpec((GW, 3 * GW), c2),
                  vec, vec, vec, vec, vec, vec, vec,
                  pl.BlockSpec((GW, GW), c2),
                  pl.BlockSpec((RWKV_CHUNK, RWKV_CHUNK), c2)],
        out_specs=pl.BlockSpec((1, rt, GW), lambda i, j: (i, j, 0)),
        out_shape=jax.ShapeDtypeStruct((b, s, GW), BF16),
        scratch_shapes=[pltpu.VMEM((GW, GW), F32), pltpu.VMEM((1, 4 * GW), F32)]
                       + [pltpu.VMEM((rt, GW), F32) for _ in range(7)],
        compiler_params=_params("parallel", "arbitrary"),
        name="rwkv7",
    )(pb, mu, wl, w0, a0, kk, ka, rk, lng, lnb, bd, ltri)


def _outproj_kernel(oa_ref, ob_ref, oc_ref, od_ref, x_ref, w_ref, g_ref, wrh_ref, wrl_ref, br_ref,
                    h_o, hn_o, eid_o, wts_o):
    mixed = jnp.concatenate([oa_ref[...], ob_ref[...], oc_ref[...], od_ref[...]], axis=1)
    h = x_ref[...] + _dot(mixed, w_ref[...])
    h_o[...] = h
    ms = jnp.mean(h * h, axis=-1, keepdims=True)
    hn = h * lax.rsqrt(ms + RMS_EPS) * g_ref[...]
    _store_row_tiles(hn_o, hn)
    hi = hn.astype(BF16)
    lo = (hn - hi.astype(F32)).astype(BF16)
    wrh = wrh_ref[...]
    logits = _dot(hi, wrh) + _dot(lo, wrh) + _dot(hi, wrl_ref[...]) + br_ref[...]
    tm = logits.shape[0]
    lane = lax.broadcasted_iota(jnp.int32, (tm, 128), 1)
    lanef = lane.astype(F32)
    big_idx = 1e9

    def rmax(t):
        return jnp.max(t, axis=1, keepdims=True)

    def rmin(t):
        return jnp.min(t, axis=1, keepdims=True)

    def rsum(t):
        return jnp.sum(t, axis=1, keepdims=True)

    gmask = lane < N_EXPERT_GROUPS
    gl = jnp.where(gmask, logits, NEG_INF)
    gmax = rmax(gl)
    gidx = rmin(jnp.where(gl == gmax, lanef, big_idx))
    gp = 1.0 / rsum(jnp.where(gmask, jnp.exp(logits - gmax), 0.0))
    e_lo = N_EXPERT_GROUPS + gidx * EXPERTS_PER_GROUP
    emask = (lanef >= e_lo) & (lanef < e_lo + EXPERTS_PER_GROUP)
    el = jnp.where(emask, logits, NEG_INF)
    emax = rmax(el)
    esum = rsum(jnp.where(emask, jnp.exp(logits - emax), 0.0))
    i1 = rmin(jnp.where(el == emax, lanef, big_idx))
    el2 = jnp.where(lanef == i1, NEG_INF, el)
    m2 = rmax(el2)
    i2 = rmin(jnp.where(el2 == m2, lanef, big_idx))
    p1 = 1.0 / esum
    p2 = jnp.exp(m2 - emax) / esum
    w1 = gp * (p1 / (p1 + p2))
    w2 = gp * (p2 / (p1 + p2))
    eid = jnp.where(lane == 0, i1 - N_EXPERT_GROUPS, jnp.where(lane == 1, i2 - N_EXPERT_GROUPS, 0.0))
    eid_o[...] = eid.astype(jnp.int32)
    wts_o[...] = jnp.where(lane == 0, w1, jnp.where(lane == 1, w2, 0.0))


def _outproj(oa, ob, oc, od, x2, w_out, g2, wr_hi, wr_lo, br):
    t = x2.shape[0]
    tm = ROW_TILE
    row = lambda i: (i, 0)
    const = lambda i: (0, 0)
    mix = pl.BlockSpec((tm, GW), row)
    return pl.pallas_call(
        _outproj_kernel,
        grid=(t // tm,),
        in_specs=[mix, mix, mix, mix,
                  pl.BlockSpec((tm, D_MODEL), row),
                  pl.BlockSpec((D_MODEL, D_MODEL), const),
                  pl.BlockSpec((1, D_MODEL), const),
                  pl.BlockSpec((D_MODEL, 128), const), pl.BlockSpec((D_MODEL, 128), const),
                  pl.BlockSpec((1, 128), const)],
        out_specs=[pl.BlockSpec((tm, D_MODEL), row), pl.BlockSpec((tm * ROW_TILE_ROWS, 128), row),
                   pl.BlockSpec((tm, 128), row), pl.BlockSpec((tm, 128), row)],
        out_shape=[jax.ShapeDtypeStruct((t, D_MODEL), F32), jax.ShapeDtypeStruct((t * ROW_TILE_ROWS, 128), F32),
                   jax.ShapeDtypeStruct((t, 128), jnp.int32), jax.ShapeDtypeStruct((t, 128), F32)],
        compiler_params=_params("parallel"),
        name="outproj_router",
    )(oa, ob, oc, od, x2, w_out, g2, wr_hi, wr_lo, br)


def _expert_kernel(blk_e_ref, blk_nv_ref, tok_ref, tokn_ref, dst_ref, hn_hbm, wg_ref, wu_ref, wd_ref, y_hbm,
                   xbuf, ybuf, wgb, wub, wdb, gsem, ssem):
    i = pl.program_id(0)
    nblk = pl.num_programs(0)
    slot = i % 2
    other = 1 - slot
    tr = ROW_TILE_ROWS
    full = xbuf.shape[1] // tr
    nv = blk_nv_ref[i]
    nv_next = jnp.where(i + 1 < nblk, blk_nv_ref[jnp.minimum(i + 1, nblk - 1)], 0)
    nv_prev = jnp.where(i >= 1, blk_nv_ref[jnp.maximum(i - 1, 0)], 0)
    nv_prev2 = jnp.where(i >= 2, blk_nv_ref[jnp.maximum(i - 2, 0)], 0)

    def tile_rows(start):
        return pl.ds(start if isinstance(start, int) else pl.multiple_of(start, tr), tr)

    def row_in(idx_ref, s, r):
        return pltpu.make_async_copy(hn_hbm.at[tile_rows(idx_ref[0, 0, r])], xbuf.at[s, tile_rows(r * tr)], gsem.at[s])

    def row_out(s, r):
        return pltpu.make_async_copy(ybuf.at[s, tile_rows(r * tr)], y_hbm.at[tile_rows(dst_ref[0, 0, r])], ssem.at[s])

    def for_rows(n, fn):
        def body(r, carry):
            fn(r)
            return carry
        lax.fori_loop(0, n, body, 0)

    def ffn():
        xb = _load_row_tiles(xbuf.at[slot], full).astype(BF16)
        hg = _dot(xb, wgb[...])
        hu = _dot(xb, wub[...])
        hdn = (hg * _sigmoid(hg) * hu).astype(BF16)
        _store_row_tiles(ybuf.at[slot], _dot(hdn, wdb[...]))

    def wait_rows(n, block_copy, row_copy):
        @pl.when(n == full)
        def _():
            block_copy.wait()

        @pl.when(n < full)
        def _():
            lax.fori_loop(0, n, lambda r, carry: (row_copy.wait(), carry)[1], 0)

    def wait_gather(n, s):
        wait_rows(n, pltpu.make_async_copy(hn_hbm.at[pl.ds(0, full * tr)], xbuf.at[s], gsem.at[s]),
                  pltpu.make_async_copy(hn_hbm.at[pl.ds(0, tr)], xbuf.at[s, pl.ds(0, tr)], gsem.at[s]))

    def wait_scatter(n, s):
        wait_rows(n, pltpu.make_async_copy(ybuf.at[s], y_hbm.at[pl.ds(0, full * tr)], ssem.at[s]),
                  pltpu.make_async_copy(ybuf.at[s, pl.ds(0, tr)], y_hbm.at[pl.ds(0, tr)], ssem.at[s]))

    @pl.when(i == 0)
    def _():
        xbuf[...] = jnp.zeros_like(xbuf)
        for_rows(nv, lambda r: row_in(tok_ref, slot, r).start())

    wait_gather(nv, slot)
    wait_scatter(nv_prev2, slot)

    @pl.when((i == 0) | (blk_e_ref[i] != blk_e_ref[jnp.maximum(i - 1, 0)]))
    def _():
        wgb[...] = wg_ref[0, 0].astype(BF16)
        wub[...] = wu_ref[0, 0].astype(BF16)
        wdb[...] = wd_ref[0, 0].astype(BF16)

    def start_rows(n, copy):
        @pl.when(n == full)
        def _():
            for r in range(full):
                copy(r).start(priority=r % 2)

        @pl.when(n < full)
        def _():
            for_rows(n, lambda r: copy(r).start())

    start_rows(nv_next, lambda r: row_in(tokn_ref, other, r))

    @pl.when(nv > 0)
    def _():
        ffn()

    start_rows(nv, lambda r: row_out(slot, r))

    @pl.when(i == nblk - 1)
    def _():
        wait_scatter(nv_prev, other)
        wait_scatter(nv, slot)


def _experts(blk_e, blk_nv, row_tok, row_dst, hn, wg, wu, wd, layer, m_rows):
    nblk = blk_e.shape[0]
    r = MOE_ROWS
    grid_spec = pltpu.PrefetchScalarGridSpec(
        num_scalar_prefetch=2,
        grid=(nblk,),
        in_specs=[pl.BlockSpec((1, 1, r), lambda i, be, nv: (i, 0, 0), memory_space=pltpu.SMEM),
                  pl.BlockSpec((1, 1, r), lambda i, be, nv: (jnp.minimum(i + 1, nblk - 1), 0, 0),
                               memory_space=pltpu.SMEM),
                  pl.BlockSpec((1, 1, r), lambda i, be, nv: (i, 0, 0), memory_space=pltpu.SMEM),
                  pl.BlockSpec(memory_space=pl.ANY),
                  pl.BlockSpec((1, 1, D_MODEL, EXPERT_FF), lambda i, be, nv: (layer, be[i], 0, 0)),
                  pl.BlockSpec((1, 1, D_MODEL, EXPERT_FF), lambda i, be, nv: (layer, be[i], 0, 0)),
                  pl.BlockSpec((1, 1, EXPERT_FF, D_MODEL), lambda i, be, nv: (layer, be[i], 0, 0))],
        out_specs=pl.BlockSpec(memory_space=pl.ANY),
        scratch_shapes=[pltpu.VMEM((2, r * ROW_TILE_ROWS, 128), F32), pltpu.VMEM((2, r * ROW_TILE_ROWS, 128), F32),
                        pltpu.VMEM((D_MODEL, EXPERT_FF), BF16), pltpu.VMEM((D_MODEL, EXPERT_FF), BF16),
                        pltpu.VMEM((EXPERT_FF, D_MODEL), BF16),
                        pltpu.SemaphoreType.DMA((2,)), pltpu.SemaphoreType.DMA((2,))],
    )
    return pl.pallas_call(
        _expert_kernel,
        grid_spec=grid_spec,
        out_shape=jax.ShapeDtypeStruct((m_rows * ROW_TILE_ROWS, 128), F32),
        compiler_params=_params("arbitrary"),
        name="experts",
    )(blk_e, blk_nv, row_tok, row_tok, row_dst, hn, wg, wu, wd)


def _dispatch(eid, t):
    r = MOE_ROWS
    m = 2 * t
    nblk = m // r + N_EXPERTS
    experts = jnp.arange(N_EXPERTS, dtype=jnp.int32)
    e_flat = eid[:, :2].reshape(m)
    counts = jnp.sum((e_flat[:, None] == experts[None, :]).astype(jnp.int32), axis=0)
    n_pad = (r - counts % r) % r
    pad_keys = jnp.where(jnp.arange(r, dtype=jnp.int32)[None, :] < n_pad[:, None], experts[:, None], N_EXPERTS)
    keys = jnp.concatenate([e_flat, pad_keys.reshape(N_EXPERTS * r)])
    payload = jnp.concatenate([jnp.arange(m, dtype=jnp.int32), jnp.full((N_EXPERTS * r,), -1, jnp.int32)])
    keys_s, a = lax.sort((keys, payload), num_keys=1, is_stable=True)
    valid = a >= 0
    row_tok = jnp.where(valid, a // 2, 0) * ROW_TILE_ROWS
    row_dst = jnp.where(valid, (a % 2) * t + a // 2, 0) * ROW_TILE_ROWS
    blk_e = jnp.minimum(keys_s.reshape(nblk, r)[:, 0], N_EXPERTS - 1)
    blk_nv = jnp.sum(valid.reshape(nblk, r).astype(jnp.int32), axis=1)
    return blk_e, blk_nv, row_tok.reshape(nblk, 1, r), row_dst.reshape(nblk, 1, r)


def _combine_kernel(h_ref, y0_ref, y1_ref, wts_ref, fg_ref, o_ref, *, final):
    w = wts_ref[...]
    rows = h_ref.shape[0]
    x = (h_ref[...] + w[:, 0:1] * _load_row_tiles(y0_ref.at[0], rows)
         + w[:, 1:2] * _load_row_tiles(y1_ref.at[0], rows))
    if final:
        ms = jnp.mean(x * x, axis=-1, keepdims=True)
        x = x * lax.rsqrt(ms + RMS_EPS) * fg_ref[...]
    o_ref[...] = x


def _combine(h, y, wts, fg, final):
    t = h.shape[0]
    tm = ROW_TILE
    row = lambda i: (i, 0)
    return pl.pallas_call(
        functools.partial(_combine_kernel, final=final),
        grid=(t // tm,),
        in_specs=[pl.BlockSpec((tm, D_MODEL), row),
                  pl.BlockSpec((1, tm * ROW_TILE_ROWS, 128), lambda i: (0, i, 0)),
                  pl.BlockSpec((1, tm * ROW_TILE_ROWS, 128), lambda i: (1, i, 0)),
                  pl.BlockSpec((tm, 128), row),
                  pl.BlockSpec((1, D_MODEL), lambda i: (0, 0))],
        out_specs=pl.BlockSpec((tm, D_MODEL), row),
        out_shape=jax.ShapeDtypeStruct((t, D_MODEL), F32),
        compiler_params=_params("parallel"),
        name="combine",
    )(h, y, y, wts, fg)


def _rope_tables(seq, head_width, rot_dim):
    half = rot_dim // 2
    inv = ROPE_THETA ** (-jnp.arange(0, rot_dim, 2, dtype=F32) / rot_dim)
    ang = jnp.arange(seq, dtype=F32)[:, None] * inv[None, :]
    cos, sin = jnp.cos(ang), jnp.sin(ang)
    d = np.arange(GW) % head_width
    idx = d % half
    cos_t = jnp.where((d < rot_dim)[None, :], cos[:, idx], 1.0)
    sin_t = jnp.where((d < half)[None, :], -sin[:, idx], jnp.where((d < rot_dim)[None, :], sin[:, idx], 0.0))
    return cos_t.astype(F32), sin_t.astype(F32)


def _swap_cols(w, head_width, rot_dim):
    half = rot_dim // 2
    c = np.arange(GW)
    d = c % head_width
    partner = np.where(d < half, c + half, np.where(d < rot_dim, c - half, c))
    keep = (d < rot_dim).astype(np.float32)
    return w[:, partner] * keep[None, :]


def _layer_weights(l, w_in, rwkv_mu, rwkv_w2, rwkv_a2, rwkv_g2, sgu_w, sgu_b, diff_l, diff_subln_g,
                   router_group_w, router_group_b, router_expert_w, router_expert_b):
    wi = w_in[l]
    a0, b0, c0, d0 = 0, 3 * GW, 3 * GW + 4 * GW, 3 * GW + 4 * GW + 2 * GW
    qa, ka, va = wi[:, a0:a0 + GW], wi[:, a0 + GW:a0 + 2 * GW], wi[:, a0 + 2 * GW:a0 + 3 * GW]
    bsl = [(0, 256), (320, 576), (576, 832), (256, 320), (832, 896), (896, 1024)]
    wb = jnp.concatenate([wi[:, b0 + s:b0 + e] for s, e in bsl], axis=1)
    mu = jnp.concatenate([rwkv_mu[l][s:e] for s, e in bsl])[None, :]
    wc = wi[:, c0:c0 + 2 * GW]
    qd, kd, vd = wi[:, d0:d0 + GW], wi[:, d0 + GW:d0 + 2 * GW], wi[:, d0 + 2 * GW:d0 + 3 * GW]
    w_all = jnp.concatenate(
        [qa, _swap_cols(qa, HEAD_DIM, 16), ka, _swap_cols(ka, HEAD_DIM, 16), wb, wc,
         qd, _swap_cols(qd, DIFF_QK_DIM, 8), kd, _swap_cols(kd, DIFF_QK_DIM, 8)], axis=1).astype(BF16)
    wvt = jnp.concatenate([va, vd], axis=1).T.astype(BF16)
    wl = jnp.zeros((GW, 3 * GW), F32)
    wl = wl.at[0:64, 0:GW].set(rwkv_w2[l]).at[64:128, GW:2 * GW].set(rwkv_a2[l]).at[128:256, 2 * GW:].set(rwkv_g2[l])
    bias2d = jnp.repeat(sgu_b[l].T, HEAD_DIM, axis=1)
    lqk = jnp.zeros((4, 128), F32).at[:, :DIFF_QK_DIM].set(jnp.stack([d[l] for d in diff_l]))
    wr = jnp.zeros((D_MODEL, 128), F32).at[:, :N_EXPERT_GROUPS].set(router_group_w[l])
    wr = wr.at[:, N_EXPERT_GROUPS:N_EXPERT_GROUPS + N_EXPERTS].set(router_expert_w[l])
    wr_hi = wr.astype(BF16)
    wr_lo = (wr - wr_hi.astype(F32)).astype(BF16)
    br = jnp.zeros((1, 128), F32).at[0, :N_EXPERT_GROUPS].set(router_group_b[l])
    br = br.at[0, N_EXPERT_GROUPS:N_EXPERT_GROUPS + N_EXPERTS].set(router_expert_b[l])
    return dict(w_all=w_all, wvt=wvt, mu=mu, wl=wl.astype(BF16), bias2d=bias2d, lqk=lqk,
                subln=jnp.tile(diff_subln_g[l], N_HEADS)[None, :], wr_hi=wr_hi, wr_lo=wr_lo, br=br)


def kernel(x, norm1_g, w_in, rwkv_mu, rwkv_w0, rwkv_w2, rwkv_a0, rwkv_a2, rwkv_g2, rwkv_kk, rwkv_ka, rwkv_rk, rwkv_ln_g, rwkv_ln_b, sgu_ln_g, sgu_ln_b, sgu_w, sgu_b, diff_lq1, diff_lk1, diff_lq2, diff_lk2, diff_subln_g, w_out, norm2_g, router_group_w, router_group_b, router_expert_w, router_expert_b, moe_w_gate, moe_w_up, moe_w_down, final_g):
    b, s, d = x.shape
    depth = w_in.shape[0]
    t = b * s
    ca, sa = _rope_tables(s, HEAD_DIM, 16)
    cd, sd = _rope_tables(s, DIFF_QK_DIM, 8)
    seg = np.arange(GW) // HEAD_DIM
    bd = jnp.asarray(seg[:, None] == seg[None, :], BF16)
    ltri = jnp.asarray(np.tril(np.ones((RWKV_CHUNK, RWKV_CHUNK))), BF16)
    row = lambda p: p[None, :]
    x2 = x.reshape(t, d)
    for l in range(depth):
        lw = _layer_weights(l, w_in, rwkv_mu, rwkv_w2, rwkv_a2, rwkv_g2, sgu_w, sgu_b,
                            (diff_lq1, diff_lk1, diff_lq2, diff_lk2), diff_subln_g,
                            router_group_w, router_group_b, router_expert_w, router_expert_b)
        qa, ka, va, pb, pc, qd, kd, vd = _inproj(x2, row(norm1_g[l]), lw["w_all"], lw["wvt"], ca, sa, cd, sd, s)
        seq3 = lambda z: z.reshape(b, s, z.shape[-1])
        out_a = _moba(seq3(qa), seq3(ka), va)
        out_b = _rwkv(seq3(pb), lw["mu"], lw["wl"], row(rwkv_w0[l]), row(rwkv_a0[l]), row(rwkv_kk[l]),
                      row(rwkv_ka[l]), row(rwkv_rk[l].reshape(GW)), row(rwkv_ln_g[l]), row(rwkv_ln_b[l]), bd, ltri)
        out_c = _sgu(pc, row(sgu_ln_g[l]), row(sgu_ln_b[l]), sgu_w[l], lw["bias2d"])
        lambda_init = 0.8 - 0.6 * math.exp(-0.3 * l)
        out_d = _diff(seq3(qd), seq3(kd), vd, lw["lqk"], lw["subln"], lambda_init)
        h, hn, eid, wts = _outproj(out_a.reshape(t, GW), out_b.reshape(t, GW), out_c, out_d.reshape(t, GW),
                                   x2, w_out[l].astype(BF16), row(norm2_g[l]), lw["wr_hi"], lw["wr_lo"], lw["br"])
        blk_e, blk_nv, row_tok, row_dst = _dispatch(eid, t)
        y = _experts(blk_e, blk_nv, row_tok, row_dst, hn, moe_w_gate, moe_w_up, moe_w_down, l, 2 * t)
        x2 = _combine(h, y.reshape(2, t * ROW_TILE_ROWS, 128), wts, row(final_g), l == depth - 1)
    return x2.reshape(b, s, d)
```

```python
import functools
import math

import numpy as np
import jax
import jax.numpy as jnp
from jax import lax
from jax.experimental import pallas as pl
from jax.experimental.pallas import tpu as pltpu

F32 = jnp.float32
BF16 = jnp.bfloat16

D_MODEL = 1024
HEAD_DIM = 64
N_HEADS = 4
GW = N_HEADS * HEAD_DIM
ROPE_THETA = 500000.0
RMS_EPS = 1e-6

ATT_BLOCK = 256
LOG2E = 1.4426950408889634
MOBA_TOPK = 3
RWKV_LN_EPS = 64e-5
RWKV_CHUNK = 64
RWKV_ROWS = 256
RWKV_INTERLEAVE = 4
SGU_CHUNK = 128
DIFF_QK_DIM = 32
DIFF_SUBLN_EPS = 1e-5

N_EXPERT_GROUPS = 4
EXPERTS_PER_GROUP = 8
N_EXPERTS = 32
EXPERT_FF = 512
MOE_ROWS = 256

ROW_TILE = 512
STREAM_TILE = 1024
VMEM_LIMIT = 56 * 1024 * 1024

NEG_INF = float("-inf")


def _dot(a, b):
    return jnp.dot(a, b, preferred_element_type=F32)


def _dot_nt(a, b):
    return lax.dot_general(a, b, (((1,), (1,)), ((), ())), preferred_element_type=F32)


def _split3(x):
    hi = x.astype(BF16)
    r1 = x - hi.astype(F32)
    mid = r1.astype(BF16)
    lo = (r1 - mid.astype(F32)).astype(BF16)
    return hi, mid, lo


def _segsum(x, bd):
    hi, mid, lo = _split3(x)
    return _dot(hi, bd) + _dot(mid, bd) + _dot(lo, bd)


def _sigmoid(x):
    return 1.0 / (1.0 + jnp.exp(-x))


def _gelu(x):
    return 0.5 * x * (1.0 + lax.erf(x * (1.0 / math.sqrt(2.0))))


ROW_TILE_ROWS = D_MODEL // 128


def _store_row_tiles(ref, x):
    rows = x.shape[0]
    for k in range(ROW_TILE_ROWS):
        ref[pl.ds(k, rows, stride=ROW_TILE_ROWS), :] = x[:, k * 128:(k + 1) * 128]


def _load_row_tiles(ref, rows):
    return jnp.concatenate([ref[pl.ds(k, rows, stride=ROW_TILE_ROWS), :] for k in range(ROW_TILE_ROWS)], axis=1)


def _params(*sem):
    return pltpu.CompilerParams(dimension_semantics=sem, vmem_limit_bytes=VMEM_LIMIT)


def _inproj_kernel(x_ref, g_ref, w_ref, wvt_ref, ca_ref, sa_ref, cd_ref, sd_ref,
                   qa_o, ka_o, vta_o, pb_o, pc_o, qd_o, kd_o, vtd_o):
    x = x_ref[...]
    ms = jnp.mean(x * x, axis=-1, keepdims=True)
    xn = (x * lax.rsqrt(ms + RMS_EPS) * g_ref[...]).astype(BF16)

    def proj(c0, n=1):
        return _dot(xn, w_ref[:, c0 * GW:(c0 + n) * GW])

    ca, sa = ca_ref[...], sa_ref[...]
    qa_o[...] = ((proj(0) * ca + proj(1) * sa) * (HEAD_DIM ** -0.5 * LOG2E)).astype(BF16)
    ka_o[...] = (proj(2) * ca + proj(3) * sa).astype(BF16)
    pb_o[...] = proj(4, 4).astype(BF16)
    pc_o[...] = proj(8, 2).astype(BF16)
    cd, sd = cd_ref[...], sd_ref[...]
    qd_o[...] = ((proj(10) * cd + proj(11) * sd) * (DIFF_QK_DIM ** -0.5 * LOG2E)).astype(BF16)
    kd_o[...] = (proj(12) * cd + proj(13) * sd).astype(BF16)
    vt = _dot_nt(wvt_ref[...], xn)
    for kb in range(x.shape[0] // ATT_BLOCK):
        cols = slice(kb * ATT_BLOCK, (kb + 1) * ATT_BLOCK)
        vta_o[kb] = vt[0:GW, cols].astype(BF16)
        vtd_o[kb] = vt[GW:2 * GW, cols].astype(BF16)


def _inproj(x2, g, w_all, wvt, ca, sa, cd, sd, seq):
    t = x2.shape[0]
    tm = min(ROW_TILE, seq)
    per_seq = seq // tm
    kbs = tm // ATT_BLOCK
    row = lambda i: (i, 0)
    tab = lambda i: (i % per_seq, 0)
    const = lambda i: (0, 0)
    vt_spec = pl.BlockSpec((kbs, GW, ATT_BLOCK), lambda i: (i, 0, 0))
    vt_shape = jax.ShapeDtypeStruct((t // ATT_BLOCK, GW, ATT_BLOCK), BF16)
    rows = lambda w: (pl.BlockSpec((tm, w), row), jax.ShapeDtypeStruct((t, w), BF16))
    outs = [rows(GW), rows(GW), (vt_spec, vt_shape), rows(4 * GW), rows(2 * GW), rows(GW), rows(GW),
            (vt_spec, vt_shape)]
    return pl.pallas_call(
        _inproj_kernel,
        grid=(t // tm,),
        in_specs=[pl.BlockSpec((tm, D_MODEL), row),
                  pl.BlockSpec((1, D_MODEL), const),
                  pl.BlockSpec((D_MODEL, 14 * GW), const),
                  pl.BlockSpec((2 * GW, D_MODEL), const),
                  pl.BlockSpec((tm, GW), tab), pl.BlockSpec((tm, GW), tab),
                  pl.BlockSpec((tm, GW), tab), pl.BlockSpec((tm, GW), tab)],
        out_specs=[o[0] for o in outs],
        out_shape=[o[1] for o in outs],
        compiler_params=_params("parallel"),
        name="inproj",
    )(x2, g, w_all, wvt, ca, sa, cd, sd)


M_INIT = -1e30


def _att_init(q, combos, qm_ref, m_ref, l_ref, acc_ref):
    lane = lax.broadcasted_iota(jnp.int32, (1, GW), 1)
    for c, (width, idx, _) in enumerate(combos):
        qm_ref[c] = jnp.where((lane // width) == idx, q, jnp.zeros_like(q))
    m_ref[...] = jnp.full_like(m_ref, M_INIT)
    l_ref[...] = jnp.zeros_like(l_ref)
    acc_ref[...] = jnp.zeros_like(acc_ref)


def _att_update(c, st, vt_sub, m_ref, l_ref, acc_ref):
    ms = slice(c * 8, c * 8 + 1)
    rows = slice(c * HEAD_DIM, (c + 1) * HEAD_DIM)
    m_old = m_ref[ms, :]
    m_new = jnp.maximum(m_old, jnp.max(st, axis=0, keepdims=True))
    alpha = jnp.exp2(m_old - m_new)
    pt = jnp.exp2(st - m_new)
    l_ref[ms, :] = alpha * l_ref[ms, :] + jnp.sum(pt, axis=0, keepdims=True)
    acc_ref[rows, :] = alpha * acc_ref[rows, :] + _dot(vt_sub, pt.astype(BF16))
    m_ref[ms, :] = m_new


def _att_keyblock(k_ref, vt_ref, n, combos, qm_ref, m_ref, l_ref, acc_ref, mask_fn):
    kb = k_ref[0, pl.ds(pl.multiple_of(n * ATT_BLOCK, ATT_BLOCK), ATT_BLOCK), :]
    st_next = _dot_nt(kb, qm_ref[0])
    for c, (_, _, head) in enumerate(combos):
        st = st_next
        if c + 1 < len(combos):
            st_next = _dot_nt(kb, qm_ref[c + 1])
        vt_sub = vt_ref[0, n, head * HEAD_DIM:(head + 1) * HEAD_DIM, :]
        _att_update(c, mask_fn(c, st), vt_sub, m_ref, l_ref, acc_ref)


def _causal_mask(c, st):
    krow = lax.broadcasted_iota(jnp.int32, st.shape, 0)
    qcol = lax.broadcasted_iota(jnp.int32, st.shape, 1)
    return jnp.where(krow <= qcol, st, NEG_INF)


def _att_result(c, l_ref, acc_ref):
    return acc_ref[c * HEAD_DIM:(c + 1) * HEAD_DIM, :] / l_ref[c * 8:c * 8 + 1, :]


def _moba_kernel(q_ref, k_ref, vt_ref, avg_ref, o_ref, qm_ref, m_ref, l_ref, acc_ref, sel_ref, ot_ref, km_ref,
                 *, nb, n_sel):
    blk = ATT_BLOCK
    j = pl.program_id(1)
    combos = [(HEAD_DIM, h, h) for h in range(N_HEADS)]
    lane = lax.broadcasted_iota(jnp.int32, (1, GW), 1)
    _att_init(q_ref[0], combos, qm_ref, m_ref, l_ref, acc_ref)

    @pl.when(j == 0)
    def _():
        kmean = _dot(avg_ref[...], k_ref[0])
        for c in range(N_HEADS):
            kmc = jnp.where((lane // HEAD_DIM) == c, kmean, 0.0)
            hi = kmc.astype(BF16)
            km_ref[c * 16:c * 16 + 8, :] = hi.astype(F32)
            km_ref[c * 16 + 8:(c + 1) * 16, :] = kmc - hi.astype(F32)

    gates = _dot_nt(km_ref[...].astype(BF16), q_ref[0])
    nrow = lax.broadcasted_iota(jnp.int32, (8, blk), 0)
    past = nrow < j
    for c in range(N_HEADS):
        gate = jnp.where(past, gates[c * 16:c * 16 + 8] + gates[c * 16 + 8:(c + 1) * 16], NEG_INF)
        rank = jnp.zeros((8, blk), F32)
        for mblk in range(nb):
            gm = gate[mblk:mblk + 1, :]
            rank = rank + jnp.where(gm > gate, 1.0, jnp.where((gm == gate) & (nrow > mblk), 1.0, 0.0))
        sel_ref[c * 8:(c + 1) * 8, :] = jnp.where(past & (rank < n_sel), 1.0, 0.0)

    def body(n, carry):
        def selected(c, st):
            return jnp.where(sel_ref[pl.ds(c * 8 + n, 1), :] > 0.5, st, NEG_INF)
        _att_keyblock(k_ref, vt_ref, n, combos, qm_ref, m_ref, l_ref, acc_ref, selected)
        return carry

    lax.fori_loop(0, j, body, 0)
    _att_keyblock(k_ref, vt_ref, j, combos, qm_ref, m_ref, l_ref, acc_ref, _causal_mask)
    for c in range(N_HEADS):
        ot_ref[c * HEAD_DIM:(c + 1) * HEAD_DIM, :] = _att_result(c, l_ref, acc_ref)
    o_ref[0] = ot_ref[...].T.astype(BF16)


def _att_specs(b, s, n_combos):
    blk = ATT_BLOCK
    nb = s // blk
    in_specs = [pl.BlockSpec((1, blk, GW), lambda i, j: (i, j, 0)),
                pl.BlockSpec((1, s, GW), lambda i, j: (i, 0, 0)),
                pl.BlockSpec((1, nb, GW, blk), lambda i, j: (i, 0, 0, 0))]
    out_spec = pl.BlockSpec((1, blk, GW), lambda i, j: (i, j, 0))
    scratch = [pltpu.VMEM((n_combos, blk, GW), BF16),
               pltpu.VMEM((n_combos * 8, blk), F32), pltpu.VMEM((n_combos * 8, blk), F32),
               pltpu.VMEM((n_combos * HEAD_DIM, blk), F32)]
    return in_specs, out_spec, scratch


def _moba(q, k, vt):
    b, s, _ = q.shape
    blk = ATT_BLOCK
    nb = s // blk
    assert nb <= 8
    n_sel = min(MOBA_TOPK, max(nb - 1, 1))
    avg = jnp.where(jnp.arange(s)[None, :] // blk == jnp.arange(8)[:, None], 1.0 / blk, 0.0).astype(BF16)
    in_specs, out_spec, scratch = _att_specs(b, s, N_HEADS)
    return pl.pallas_call(
        functools.partial(_moba_kernel, nb=nb, n_sel=n_sel),
        grid=(b, nb),
        in_specs=in_specs + [pl.BlockSpec((8, s), lambda i, j: (0, 0))],
        out_specs=out_spec,
        out_shape=jax.ShapeDtypeStruct((b, s, GW), BF16),
        scratch_shapes=scratch + [pltpu.VMEM((N_HEADS * 8, blk), F32), pltpu.VMEM((GW, blk), F32),
                                  pltpu.VMEM((N_HEADS * 16, GW), F32)],
        compiler_params=_params("parallel", "arbitrary"),
        name="moba",
    )(q, k, vt.reshape(b, nb, GW, blk), avg)


def _diff_kernel(q_ref, k_ref, vt_ref, lqk_ref, g_ref, o_ref, qm_ref, m_ref, l_ref, acc_ref, ot_ref,
                 *, lambda_init):
    j = pl.program_id(1)
    combos = [(DIFF_QK_DIM, 2 * h + i, h) for h in range(N_HEADS) for i in range(2)]
    _att_init(q_ref[0], combos, qm_ref, m_ref, l_ref, acc_ref)

    def body(n, carry):
        _att_keyblock(k_ref, vt_ref, n, combos, qm_ref, m_ref, l_ref, acc_ref, lambda c, st: st)
        return carry

    lax.fori_loop(0, j, body, 0)
    _att_keyblock(k_ref, vt_ref, j, combos, qm_ref, m_ref, l_ref, acc_ref, _causal_mask)

    lqk = lqk_ref[...]
    lam = (jnp.exp(jnp.sum(lqk[0:1] * lqk[1:2], axis=1, keepdims=True))
           - jnp.exp(jnp.sum(lqk[2:3] * lqk[3:4], axis=1, keepdims=True)) + lambda_init)
    for h in range(N_HEADS):
        x = _att_result(2 * h, l_ref, acc_ref) - lam * _att_result(2 * h + 1, l_ref, acc_ref)
        ms = jnp.mean(x * x, axis=0, keepdims=True)
        ot_ref[h * HEAD_DIM:(h + 1) * HEAD_DIM, :] = x * lax.rsqrt(ms + DIFF_SUBLN_EPS)
    o_ref[0] = (ot_ref[...].T * g_ref[...] * (1.0 - lambda_init)).astype(BF16)


def _diff(q, k, vt, lqk, g_tiled, lambda_init):
    b, s, _ = q.shape
    blk = ATT_BLOCK
    nb = s // blk
    in_specs, out_spec, scratch = _att_specs(b, s, 2 * N_HEADS)
    return pl.pallas_call(
        functools.partial(_diff_kernel, lambda_init=lambda_init),
        grid=(b, nb),
        in_specs=in_specs + [pl.BlockSpec((4, 128), lambda i, j: (0, 0)),
                             pl.BlockSpec((1, GW), lambda i, j: (0, 0))],
        out_specs=out_spec,
        out_shape=jax.ShapeDtypeStruct((b, s, GW), BF16),
        scratch_shapes=scratch + [pltpu.VMEM((GW, blk), F32)],
        compiler_params=_params("parallel", "arbitrary"),
        name="diffattn",
    )(q, k, vt.reshape(b, nb, GW, blk), lqk, g_tiled)


def _sgu_kernel(pc_ref, lng_ref, lnb_ref, w_ref, bias_ref, o_ref):
    rows = pc_ref.shape[0]
    pc = pc_ref[...].astype(F32)
    u = _gelu(pc[:, :GW])
    v = _gelu(pc[:, GW:])
    mu = jnp.mean(v, axis=-1, keepdims=True)
    d = v - mu
    var = jnp.mean(d * d, axis=-1, keepdims=True)
    vn = d * lax.rsqrt(var + 1e-5) * lng_ref[...] + lnb_ref[...]
    lane = lax.broadcasted_iota(jnp.int32, (1, GW), 1)
    tr = lax.broadcasted_iota(jnp.int32, (SGU_CHUNK, SGU_CHUNK), 0)
    tc = lax.broadcasted_iota(jnp.int32, (SGU_CHUNK, SGU_CHUNK), 1)
    wm = [jnp.where(tr >= tc, w_ref[h], 0.0).astype(BF16) for h in range(N_HEADS)]
    for c in range(rows // SGU_CHUNK):
        sl = slice(c * SGU_CHUNK, (c + 1) * SGU_CHUNK)
        vc = vn[sl]
        mixed = bias_ref[...]
        for h in range(N_HEADS):
            vm = jnp.where((lane // HEAD_DIM) == h, vc, 0.0).astype(BF16)
            mixed = mixed + _dot(wm[h], vm)
        o_ref[sl, :] = (u[sl] * mixed).astype(BF16)


def _sgu(pc, lng, lnb, w, bias2d):
    t = pc.shape[0]
    tm = ROW_TILE
    const2 = lambda i: (0, 0)
    return pl.pallas_call(
        _sgu_kernel,
        grid=(t // tm,),
        in_specs=[pl.BlockSpec((tm, 2 * GW), lambda i: (i, 0)),
                  pl.BlockSpec((1, GW), const2), pl.BlockSpec((1, GW), const2),
                  pl.BlockSpec((N_HEADS, SGU_CHUNK, SGU_CHUNK), lambda i: (0, 0, 0)),
                  pl.BlockSpec((SGU_CHUNK, GW), const2)],
        out_specs=pl.BlockSpec((tm, GW), lambda i: (i, 0)),
        out_shape=jax.ShapeDtypeStruct((t, GW), BF16),
        compiler_params=_params("parallel"),
        name="sgu",
    )(pc, lng, lnb, w, bias2d)


def _rwkv_kernel(pb_ref, mu_ref, wl_ref, w0_ref, a0_ref, kk_ref, ka_ref, rk_ref, lng_ref, lnb_ref,
                 bd_ref, ltri_ref, o_ref,
                 state_ref, prev_ref, r_s, lw_s, k_s, v_s, al_s, be_s, y_s):
    rt = pb_ref.shape[1]
    c_len = RWKV_CHUNK

    @pl.when(pl.program_id(1) == 0)
    def _():
        state_ref[...] = jnp.zeros_like(state_ref)
        prev_ref[...] = jnp.zeros_like(prev_ref)

    bd = bd_ref[...]
    xt = pb_ref[0].astype(F32)
    rowid = lax.broadcasted_iota(jnp.int32, (rt, 1), 0)
    prev = jnp.where(rowid == 0, prev_ref[...], pltpu.roll(xt, 1, axis=0))
    prev_ref[...] = xt[rt - 1:rt, :]
    xs = xt + (prev - xt) * mu_ref[...]
    r = xs[:, 0:GW]
    k = xs[:, GW:2 * GW]
    v = xs[:, 2 * GW:3 * GW]
    lo = xs[:, 3 * GW:4 * GW]
    lane = lax.broadcasted_iota(jnp.int32, (1, GW), 1)
    lin = jnp.where(lane < 64, jnp.tanh(lo), jnp.where(lane < 128, lo, _sigmoid(lo)))
    z = _dot(lin.astype(BF16), wl_ref[...])
    wz = -(w0_ref[...] + z[:, 0:GW])
    softplus = jnp.maximum(wz, 0.0) + jnp.log(1.0 + jnp.exp(-jnp.abs(wz)))
    lw = -jnp.exp(-softplus - 0.5)
    a = _sigmoid(a0_ref[...] + z[:, GW:2 * GW])
    gate = z[:, 2 * GW:3 * GW]
    kk = k * kk_ref[...]
    kk = kk / jnp.maximum(jnp.sqrt(_segsum(kk * kk, bd)), 1e-12)
    k2 = k * (1.0 + (a - 1.0) * ka_ref[...])
    bonus = _segsum(r * k2 * rk_ref[...], bd) * v

    r_s[...] = r
    lw_s[...] = lw
    k_s[...] = k2
    v_s[...] = v
    al_s[...] = -kk
    be_s[...] = kk * a

    hms = [(lane // HEAD_DIM) == h for h in range(N_HEADS)]
    brow = lax.broadcasted_iota(jnp.int32, (GW, GW), 0)
    bcol = lax.broadcasted_iota(jnp.int32, (GW, GW), 1)
    strict = brow > bcol
    lower = brow >= bcol
    eye = brow == bcol
    ltri = ltri_ref[...]

    def big(x):
        return jnp.concatenate([jnp.where(hm, x, 0.0) for hm in hms], axis=0)

    def unbig(x):
        return x[0:c_len] + x[c_len:2 * c_len] + x[2 * c_len:3 * c_len] + x[3 * c_len:4 * c_len]

    def chunk_terms(sl, out):
        lwc = lw_s[sl, :]
        h3 = _split3(lwc)
        cs = _dot(ltri, h3[0]) + _dot(ltri, h3[1]) + _dot(ltri, h3[2])
        cs_end = cs[c_len - 1:c_len, :]
        e_neg = jnp.exp(-cs)
        e_rem = jnp.exp(cs_end - cs)
        al, be, kc, rc, vc = al_s[sl, :], be_s[sl, :], k_s[sl, :], r_s[sl, :], v_s[sl, :]
        r_t = rc * jnp.exp(cs)
        a_b = big(al * jnp.exp(cs - lwc)).astype(BF16)
        r_b = big(r_t).astype(BF16)
        bi_b = big(be * e_neg).astype(BF16)
        ki_b = big(kc * e_neg).astype(BF16)
        v_b = big(vc).astype(BF16)
        g = _dot_nt(jnp.concatenate([a_b, r_b], axis=0), jnp.concatenate([bi_b, ki_b], axis=0))
        yield
        a_ab = jnp.where(strict, g[:GW, :GW], 0.0)
        a_ak = jnp.where(strict, g[:GW, GW:], 0.0).astype(BF16)
        a_rb = jnp.where(lower, g[GW:, :GW], 0.0).astype(BF16)
        a_rk = jnp.where(lower, g[GW:, GW:], 0.0).astype(BF16)
        xb = a_ab.astype(BF16)
        tinv = jnp.where(eye, 1.0, a_ab)
        pb = _dot(xb, xb).astype(BF16)
        yield
        for _ in range(4):
            both = _dot(jnp.concatenate([pb, tinv.astype(BF16)], axis=0), pb)
            yield
            tinv = tinv + both[GW:]
            pb = both[:GW].astype(BF16)
        tb = (tinv + _dot(tinv.astype(BF16), pb)).astype(BF16)
        yield
        bd_t = big(be * e_rem).T.astype(BF16)
        kd_t = big(kc * e_rem).T.astype(BF16)
        on_v = _dot(jnp.concatenate([a_ak, a_rk, kd_t], axis=0), v_b)
        yield
        p1 = _dot(tb, a_b).astype(BF16)
        p2 = _dot(tb, on_v[:GW].astype(BF16)).astype(BF16)
        yield
        on_p = _dot(jnp.concatenate([a_rb, bd_t], axis=0), jnp.concatenate([p1, p2], axis=1))
        out.append((r_t + unbig(on_p[:GW, :GW]),
                    unbig(on_p[:GW, GW:] + on_v[GW:2 * GW]),
                    jnp.where(eye, jnp.exp(cs_end), 0.0) + on_p[GW:, :GW],
                    on_p[GW:, GW:] + on_v[2 * GW:]))
        yield

    def chunk_group(i, carry):
        slices = [pl.ds(pl.multiple_of((RWKV_INTERLEAVE * i + k) * c_len, c_len), c_len)
                  for k in range(RWKV_INTERLEAVE)]
        outs = [[] for _ in slices]
        stages = [chunk_terms(sl, out) for sl, out in zip(slices, outs)]
        for _ in zip(*stages):
            pass
        for sl, out in zip(slices, outs):
            q_tok, z_tok, m_c, n_c = out[0]
            skb = state_ref[...].astype(BF16)
            y_s[sl, :] = _dot(q_tok.astype(BF16), skb) + z_tok
            state_ref[...] = _dot(m_c.astype(BF16), skb) + n_c
        return carry

    lax.fori_loop(0, rt // (RWKV_INTERLEAVE * c_len), chunk_group, 0)

    y = y_s[...]
    mean = _segsum(y, bd) * (1.0 / HEAD_DIM)
    d = y - mean
    var = _segsum(d * d, bd) * (1.0 / HEAD_DIM)
    yn = d * lax.rsqrt(var + RWKV_LN_EPS) * lng_ref[...] + lnb_ref[...]
    o_ref[0] = ((yn + bonus) * gate).astype(BF16)


def _rwkv(pb, mu, wl, w0, a0, kk, ka, rk, lng, lnb, bd, ltri):
    b, s, _ = pb.shape
    rt = RWKV_ROWS
    c2 = lambda i, j: (0, 0)
    vec = pl.BlockSpec((1, GW), c2)
    return pl.pallas_call(
        _rwkv_kernel,
        grid=(b, s // rt),
        in_specs=[pl.BlockSpec((1, rt, 4 * GW), lambda i, j: (i, j, 0)),
                  pl.BlockSpec((1, 4 * GW), c2),
                  pl.BlockSpec((GW, 3 * GW), c2),
                  vec, vec, vec, vec, vec, vec, vec,
                  pl.BlockSpec((GW, GW), c2),
                  pl.BlockSpec((RWKV_CHUNK, RWKV_CHUNK), c2)],
        out_specs=pl.BlockSpec((1, rt, GW), lambda i, j: (i, j, 0)),
        out_shape=jax.ShapeDtypeStruct((b, s, GW), BF16),
        scratch_shapes=[pltpu.VMEM((GW, GW), F32), pltpu.VMEM((1, 4 * GW), F32)]
                       + [pltpu.VMEM((rt, GW), F32) for _ in range(7)],
        compiler_params=_params("parallel", "arbitrary"),
        name="rwkv7",
    )(pb, mu, wl, w0, a0, kk, ka, rk, lng, lnb, bd, ltri)


def _outproj_kernel(oa_ref, ob_ref, oc_ref, od_ref, x_ref, w_ref, g_ref, wrh_ref, wrl_ref, br_ref,
                    h_o, hn_o, eid_o, wts_o):
    mixed = jnp.concatenate([oa_ref[...], ob_ref[...], oc_ref[...], od_ref[...]], axis=1)
    h = x_ref[...] + _dot(mixed, w_ref[...])
    h_o[...] = h
    ms = jnp.mean(h * h, axis=-1, keepdims=True)
    hn = h * lax.rsqrt(ms + RMS_EPS) * g_ref[...]
    _store_row_tiles(hn_o, hn)
    hi = hn.astype(BF16)
    lo = (hn - hi.astype(F32)).astype(BF16)
    wrh = wrh_ref[...]
    logits = _dot(hi, wrh) + _dot(lo, wrh) + _dot(hi, wrl_ref[...]) + br_ref[...]
    tm = logits.shape[0]
    lane = lax.broadcasted_iota(jnp.int32, (tm, 128), 1)
    lanef = lane.astype(F32)
    big_idx = 1e9

    def rmax(t):
        return jnp.max(t, axis=1, keepdims=True)

    def rmin(t):
        return jnp.min(t, axis=1, keepdims=True)

    def rsum(t):
        return jnp.sum(t, axis=1, keepdims=True)

    gmask = lane < N_EXPERT_GROUPS
    gl = jnp.where(gmask, logits, NEG_INF)
    gmax = rmax(gl)
    gidx = rmin(jnp.where(gl == gmax, lanef, big_idx))
    gp = 1.0 / rsum(jnp.where(gmask, jnp.exp(logits - gmax), 0.0))
    e_lo = N_EXPERT_GROUPS + gidx * EXPERTS_PER_GROUP
    emask = (lanef >= e_lo) & (lanef < e_lo + EXPERTS_PER_GROUP)
    el = jnp.where(emask, logits, NEG_INF)
    emax = rmax(el)
    esum = rsum(jnp.where(emask, jnp.exp(logits - emax), 0.0))
    i1 = rmin(jnp.where(el == emax, lanef, big_idx))
    el2 = jnp.where(lanef == i1, NEG_INF, el)
    m2 = rmax(el2)
    i2 = rmin(jnp.where(el2 == m2, lanef, big_idx))
    p1 = 1.0 / esum
    p2 = jnp.exp(m2 - emax) / esum
    w1 = gp * (p1 / (p1 + p2))
    w2 = gp * (p2 / (p1 + p2))
    eid = jnp.where(lane == 0, i1 - N_EXPERT_GROUPS, jnp.where(lane == 1, i2 - N_EXPERT_GROUPS, 0.0))
    eid_o[...] = eid.astype(jnp.int32)
    wts_o[...] = jnp.where(lane == 0, w1, jnp.where(lane == 1, w2, 0.0))


def _outproj(oa, ob, oc, od, x2, w_out, g2, wr_hi, wr_lo, br):
    t = x2.shape[0]
    tm = STREAM_TILE
    row = lambda i: (i, 0)
    const = lambda i: (0, 0)
    mix = pl.BlockSpec((tm, GW), row)
    return pl.pallas_call(
        _outproj_kernel,
        grid=(t // tm,),
        in_specs=[mix, mix, mix, mix,
                  pl.BlockSpec((tm, D_MODEL), row),
                  pl.BlockSpec((D_MODEL, D_MODEL), const),
                  pl.BlockSpec((1, D_MODEL), const),
                  pl.BlockSpec((D_MODEL, 128), const), pl.BlockSpec((D_MODEL, 128), const),
                  pl.BlockSpec((1, 128), const)],
        out_specs=[pl.BlockSpec((tm, D_MODEL), row), pl.BlockSpec((tm * ROW_TILE_ROWS, 128), row),
                   pl.BlockSpec((tm, 128), row), pl.BlockSpec((tm, 128), row)],
        out_shape=[jax.ShapeDtypeStruct((t, D_MODEL), F32), jax.ShapeDtypeStruct((t * ROW_TILE_ROWS, 128), F32),
                   jax.ShapeDtypeStruct((t, 128), jnp.int32), jax.ShapeDtypeStruct((t, 128), F32)],
        compiler_params=_params("parallel"),
        name="outproj_router",
    )(oa, ob, oc, od, x2, w_out, g2, wr_hi, wr_lo, br)


def _expert_kernel(blk_e_ref, blk_nv_ref, tok_ref, tokn_ref, dst_ref, hn_hbm, wg_ref, wu_ref, wd_ref, y_hbm,
                   xbuf, ybuf, wgb, wub, wdb, gsem, ssem):
    i = pl.program_id(0)
    nblk = pl.num_programs(0)
    slot = i % 2
    other = 1 - slot
    tr = ROW_TILE_ROWS
    full = xbuf.shape[1] // tr
    nv = blk_nv_ref[i]
    nv_next = jnp.where(i + 1 < nblk, blk_nv_ref[jnp.minimum(i + 1, nblk - 1)], 0)
    nv_prev = jnp.where(i >= 1, blk_nv_ref[jnp.maximum(i - 1, 0)], 0)
    nv_prev2 = jnp.where(i >= 2, blk_nv_ref[jnp.maximum(i - 2, 0)], 0)

    def tile_rows(start):
        return pl.ds(start if isinstance(start, int) else pl.multiple_of(start, tr), tr)

    def row_in(idx_ref, s, r):
        return pltpu.make_async_copy(hn_hbm.at[tile_rows(idx_ref[0, 0, r])], xbuf.at[s, tile_rows(r * tr)], gsem.at[s])

    def row_out(s, r):
        return pltpu.make_async_copy(ybuf.at[s, tile_rows(r * tr)], y_hbm.at[tile_rows(dst_ref[0, 0, r])], ssem.at[s])

    def for_rows(n, fn):
        def body(r, carry):
            fn(r)
            return carry
        lax.fori_loop(0, n, body, 0)

    def ffn():
        xb = _load_row_tiles(xbuf.at[slot], full).astype(BF16)
        hg = _dot(xb, wgb[...])
        hu = _dot(xb, wub[...])
        hdn = (hg * _sigmoid(hg) * hu).astype(BF16)
        _store_row_tiles(ybuf.at[slot], _dot(hdn, wdb[...]))

    def wait_rows(n, block_copy, row_copy):
        @pl.when(n == full)
        def _():
            block_copy.wait()

        @pl.when(n < full)
        def _():
            lax.fori_loop(0, n, lambda r, carry: (row_copy.wait(), carry)[1], 0)

    def wait_gather(n, s):
        wait_rows(n, pltpu.make_async_copy(hn_hbm.at[pl.ds(0, full * tr)], xbuf.at[s], gsem.at[s]),
                  pltpu.make_async_copy(hn_hbm.at[pl.ds(0, tr)], xbuf.at[s, pl.ds(0, tr)], gsem.at[s]))

    def wait_scatter(n, s):
        wait_rows(n, pltpu.make_async_copy(ybuf.at[s], y_hbm.at[pl.ds(0, full * tr)], ssem.at[s]),
                  pltpu.make_async_copy(ybuf.at[s, pl.ds(0, tr)], y_hbm.at[pl.ds(0, tr)], ssem.at[s]))

    @pl.when(i == 0)
    def _():
        xbuf[...] = jnp.zeros_like(xbuf)
        for_rows(nv, lambda r: row_in(tok_ref, slot, r).start())

    wait_gather(nv, slot)
    wait_scatter(nv_prev2, slot)

    @pl.when((i == 0) | (blk_e_ref[i] != blk_e_ref[jnp.maximum(i - 1, 0)]))
    def _():
        wgb[...] = wg_ref[0, 0].astype(BF16)
        wub[...] = wu_ref[0, 0].astype(BF16)
        wdb[...] = wd_ref[0, 0].astype(BF16)

    def start_rows(n, copy):
        @pl.when(n == full)
        def _():
            for r in range(full):
                copy(r).start(priority=r % 2)

        @pl.when(n < full)
        def _():
            for_rows(n, lambda r: copy(r).start())

    start_rows(nv_next, lambda r: row_in(tokn_ref, other, r))

    @pl.when(nv > 0)
    def _():
        ffn()

    start_rows(nv, lambda r: row_out(slot, r))

    @pl.when(i == nblk - 1)
    def _():
        wait_scatter(nv_prev, other)
        wait_scatter(nv, slot)


def _experts(blk_e, blk_nv, row_tok, row_dst, hn, wg, wu, wd, layer, m_rows):
    nblk = blk_e.shape[0]
    r = MOE_ROWS
    grid_spec = pltpu.PrefetchScalarGridSpec(
        num_scalar_prefetch=2,
        grid=(nblk,),
        in_specs=[pl.BlockSpec((1, 1, r), lambda i, be, nv: (i, 0, 0), memory_space=pltpu.SMEM),
                  pl.BlockSpec((1, 1, r), lambda i, be, nv: (jnp.minimum(i + 1, nblk - 1), 0, 0),
                               memory_space=pltpu.SMEM),
                  pl.BlockSpec((1, 1, r), lambda i, be, nv: (i, 0, 0), memory_space=pltpu.SMEM),
                  pl.BlockSpec(memory_space=pl.ANY),
                  pl.BlockSpec((1, 1, D_MODEL, EXPERT_FF), lambda i, be, nv: (layer, be[i], 0, 0)),
                  pl.BlockSpec((1, 1, D_MODEL, EXPERT_FF), lambda i, be, nv: (layer, be[i], 0, 0)),
                  pl.BlockSpec((1, 1, EXPERT_FF, D_MODEL), lambda i, be, nv: (layer, be[i], 0, 0))],
        out_specs=pl.BlockSpec(memory_space=pl.ANY),
        scratch_shapes=[pltpu.VMEM((2, r * ROW_TILE_ROWS, 128), F32), pltpu.VMEM((2, r * ROW_TILE_ROWS, 128), F32),
                        pltpu.VMEM((D_MODEL, EXPERT_FF), BF16), pltpu.VMEM((D_MODEL, EXPERT_FF), BF16),
                        pltpu.VMEM((EXPERT_FF, D_MODEL), BF16),
                        pltpu.SemaphoreType.DMA((2,)), pltpu.SemaphoreType.DMA((2,))],
    )
    return pl.pallas_call(
        _expert_kernel,
        grid_spec=grid_spec,
        out_shape=jax.ShapeDtypeStruct((m_rows * ROW_TILE_ROWS, 128), F32),
        compiler_params=_params("arbitrary"),
        name="experts",
    )(blk_e, blk_nv, row_tok, row_tok, row_dst, hn, wg, wu, wd)


def _dispatch(eid, t):
    r = MOE_ROWS
    m = 2 * t
    nblk = m // r + N_EXPERTS
    experts = jnp.arange(N_EXPERTS, dtype=jnp.int32)
    e_flat = eid[:, :2].reshape(m)
    counts = jnp.sum((e_flat[:, None] == experts[None, :]).astype(jnp.int32), axis=0)
    n_pad = (r - counts % r) % r
    pad_keys = jnp.where(jnp.arange(r, dtype=jnp.int32)[None, :] < n_pad[:, None], experts[:, None], N_EXPERTS)
    keys = jnp.concatenate([e_flat, pad_keys.reshape(N_EXPERTS * r)])
    payload = jnp.concatenate([jnp.arange(m, dtype=jnp.int32), jnp.full((N_EXPERTS * r,), -1, jnp.int32)])
    keys_s, a = lax.sort((keys, payload), num_keys=1, is_stable=True)
    valid = a >= 0
    row_tok = jnp.where(valid, a // 2, 0) * ROW_TILE_ROWS
    row_dst = jnp.where(valid, (a % 2) * t + a // 2, 0) * ROW_TILE_ROWS
    blk_e = jnp.minimum(keys_s.reshape(nblk, r)[:, 0], N_EXPERTS - 1)
    blk_nv = jnp.sum(valid.reshape(nblk, r).astype(jnp.int32), axis=1)
    return blk_e, blk_nv, row_tok.reshape(nblk, 1, r), row_dst.reshape(nblk, 1, r)


def _combine_kernel(h_ref, y0_ref, y1_ref, wts_ref, fg_ref, o_ref, *, final):
    w = wts_ref[...]
    rows = h_ref.shape[0]
    x = (h_ref[...] + w[:, 0:1] * _load_row_tiles(y0_ref.at[0], rows)
         + w[:, 1:2] * _load_row_tiles(y1_ref.at[0], rows))
    if final:
        ms = jnp.mean(x * x, axis=-1, keepdims=True)
        x = x * lax.rsqrt(ms + RMS_EPS) * fg_ref[...]
    o_ref[...] = x


def _combine(h, y, wts, fg, final):
    t = h.shape[0]
    tm = STREAM_TILE
    row = lambda i: (i, 0)
    return pl.pallas_call(
        functools.partial(_combine_kernel, final=final),
        grid=(t // tm,),
        in_specs=[pl.BlockSpec((tm, D_MODEL), row),
                  pl.BlockSpec((1, tm * ROW_TILE_ROWS, 128), lambda i: (0, i, 0)),
                  pl.BlockSpec((1, tm * ROW_TILE_ROWS, 128), lambda i: (1, i, 0)),
                  pl.BlockSpec((tm, 128), row),
                  pl.BlockSpec((1, D_MODEL), lambda i: (0, 0))],
        out_specs=pl.BlockSpec((tm, D_MODEL), row),
        out_shape=jax.ShapeDtypeStruct((t, D_MODEL), F32),
        compiler_params=_params("parallel"),
        name="combine",
    )(h, y, y, wts, fg)


def _rope_tables(seq, head_width, rot_dim):
    half = rot_dim // 2
    inv = ROPE_THETA ** (-jnp.arange(0, rot_dim, 2, dtype=F32) / rot_dim)
    ang = jnp.arange(seq, dtype=F32)[:, None] * inv[None, :]
    cos, sin = jnp.cos(ang), jnp.sin(ang)
    d = np.arange(GW) % head_width
    idx = d % half
    cos_t = jnp.where((d < rot_dim)[None, :], cos[:, idx], 1.0)
    sin_t = jnp.where((d < half)[None, :], -sin[:, idx], jnp.where((d < rot_dim)[None, :], sin[:, idx], 0.0))
    return cos_t.astype(F32), sin_t.astype(F32)


def _swap_cols(w, head_width, rot_dim):
    half = rot_dim // 2
    c = np.arange(GW)
    d = c % head_width
    partner = np.where(d < half, c + half, np.where(d < rot_dim, c - half, c))
    keep = (d < rot_dim).astype(np.float32)
    return w[:, partner] * keep[None, :]


def _layer_weights(l, w_in, rwkv_mu, rwkv_w2, rwkv_a2, rwkv_g2, sgu_w, sgu_b, diff_l, diff_subln_g,
                   router_group_w, router_group_b, router_expert_w, router_expert_b):
    wi = w_in[l]
    a0, b0, c0, d0 = 0, 3 * GW, 3 * GW + 4 * GW, 3 * GW + 4 * GW + 2 * GW
    qa, ka, va = wi[:, a0:a0 + GW], wi[:, a0 + GW:a0 + 2 * GW], wi[:, a0 + 2 * GW:a0 + 3 * GW]
    bsl = [(0, 256), (320, 576), (576, 832), (256, 320), (832, 896), (896, 1024)]
    wb = jnp.concatenate([wi[:, b0 + s:b0 + e] for s, e in bsl], axis=1)
    mu = jnp.concatenate([rwkv_mu[l][s:e] for s, e in bsl])[None, :]
    wc = wi[:, c0:c0 + 2 * GW]
    qd, kd, vd = wi[:, d0:d0 + GW], wi[:, d0 + GW:d0 + 2 * GW], wi[:, d0 + 2 * GW:d0 + 3 * GW]
    w_all = jnp.concatenate(
        [qa, _swap_cols(qa, HEAD_DIM, 16), ka, _swap_cols(ka, HEAD_DIM, 16), wb, wc,
         qd, _swap_cols(qd, DIFF_QK_DIM, 8), kd, _swap_cols(kd, DIFF_QK_DIM, 8)], axis=1).astype(BF16)
    wvt = jnp.concatenate([va, vd], axis=1).T.astype(BF16)
    wl = jnp.zeros((GW, 3 * GW), F32)
    wl = wl.at[0:64, 0:GW].set(rwkv_w2[l]).at[64:128, GW:2 * GW].set(rwkv_a2[l]).at[128:256, 2 * GW:].set(rwkv_g2[l])
    bias2d = jnp.repeat(sgu_b[l].T, HEAD_DIM, axis=1)
    lqk = jnp.zeros((4, 128), F32).at[:, :DIFF_QK_DIM].set(jnp.stack([d[l] for d in diff_l]))
    wr = jnp.zeros((D_MODEL, 128), F32).at[:, :N_EXPERT_GROUPS].set(router_group_w[l])
    wr = wr.at[:, N_EXPERT_GROUPS:N_EXPERT_GROUPS + N_EXPERTS].set(router_expert_w[l])
    wr_hi = wr.astype(BF16)
    wr_lo = (wr - wr_hi.astype(F32)).astype(BF16)
    br = jnp.zeros((1, 128), F32).at[0, :N_EXPERT_GROUPS].set(router_group_b[l])
    br = br.at[0, N_EXPERT_GROUPS:N_EXPERT_GROUPS + N_EXPERTS].set(router_expert_b[l])
    return dict(w_all=w_all, wvt=wvt, mu=mu, wl=wl.astype(BF16), bias2d=bias2d, lqk=lqk,
                subln=jnp.tile(diff_subln_g[l], N_HEADS)[None, :], wr_hi=wr_hi, wr_lo=wr_lo, br=br)


def kernel(x, norm1_g, w_in, rwkv_mu, rwkv_w0, rwkv_w2, rwkv_a0, rwkv_a2, rwkv_g2, rwkv_kk, rwkv_ka, rwkv_rk, rwkv_ln_g, rwkv_ln_b, sgu_ln_g, sgu_ln_b, sgu_w, sgu_b, diff_lq1, diff_lk1, diff_lq2, diff_lk2, diff_subln_g, w_out, norm2_g, router_group_w, router_group_b, router_expert_w, router_expert_b, moe_w_gate, moe_w_up, moe_w_down, final_g):
    b, s, d = x.shape
    depth = w_in.shape[0]
    t = b * s
    ca, sa = _rope_tables(s, HEAD_DIM, 16)
    cd, sd = _rope_tables(s, DIFF_QK_DIM, 8)
    seg = np.arange(GW) // HEAD_DIM
    bd = jnp.asarray(seg[:, None] == seg[None, :], BF16)
    ltri = jnp.asarray(np.tril(np.ones((RWKV_CHUNK, RWKV_CHUNK))), BF16)
    row = lambda p: p[None, :]
    x2 = x.reshape(t, d)
    for l in range(depth):
        lw = _layer_weights(l, w_in, rwkv_mu, rwkv_w2, rwkv_a2, rwkv_g2, sgu_w, sgu_b,
                            (diff_lq1, diff_lk1, diff_lq2, diff_lk2), diff_subln_g,
                            router_group_w, router_group_b, router_expert_w, router_expert_b)
        qa, ka, va, pb, pc, qd, kd, vd = _inproj(x2, row(norm1_g[l]), lw["w_all"], lw["wvt"], ca, sa, cd, sd, s)
        seq3 = lambda z: z.reshape(b, s, z.shape[-1])
        out_a = _moba(seq3(qa), seq3(ka), va)
        out_b = _rwkv(seq3(pb), lw["mu"], lw["wl"], row(rwkv_w0[l]), row(rwkv_a0[l]), row(rwkv_kk[l]),
                      row(rwkv_ka[l]), row(rwkv_rk[l].reshape(GW)), row(rwkv_ln_g[l]), row(rwkv_ln_b[l]), bd, ltri)
        out_c = _sgu(pc, row(sgu_ln_g[l]), row(sgu_ln_b[l]), sgu_w[l], lw["bias2d"])
        lambda_init = 0.8 - 0.6 * math.exp(-0.3 * l)
        out_d = _diff(seq3(qd), seq3(kd), vd, lw["lqk"], lw["subln"], lambda_init)
        h, hn, eid, wts = _outproj(out_a.reshape(t, GW), out_b.reshape(t, GW), out_c, out_d.reshape(t, GW),
                                   x2, w_out[l].astype(BF16), row(norm2_g[l]), lw["wr_hi"], lw["wr_lo"], lw["br"])
        blk_e, blk_nv, row_tok, row_dst = _dispatch(eid, t)
        y = _experts(blk_e, blk_nv, row_tok, row_dst, hn, moe_w_gate, moe_w_up, moe_w_down, l, 2 * t)
        x2 = _combine(h, y.reshape(2, t * ROW_TILE_ROWS, 128), wts, row(final_g), l == depth - 1)
    return x2.reshape(b, s, d)
```
